```python
import math
import jax
import jax.numpy as jnp
from jax import lax
import numpy as np

D_MODEL = 4096
BATCH = 4
SEQ = 2048
DEPTH = 2
DEC_BATCH = 128
DEC_SEQ = 4
PAST_LEN = 16384
PAGE_SIZE = 128

N_MIXERS = 2
N_GDN = (DEPTH + 1) // 2
N_MLA = DEPTH // 2
EPS = 1e-6
D_FF = 11008
GDN_K_HEADS = 16
GDN_V_HEADS = 32
GDN_HEAD_DK = 128
GDN_HEAD_DV = 128
GDN_CONV = 4
GDN_CHUNK = 64
GDN_QK_DIM = GDN_K_HEADS * GDN_HEAD_DK
GDN_V_DIM = GDN_V_HEADS * GDN_HEAD_DV
GDN_CONV_DIM = 2 * GDN_QK_DIM + GDN_V_DIM
GDN_IN_DIM = GDN_CONV_DIM + GDN_V_DIM + 2 * GDN_V_HEADS
MLA_HEADS = 64
MLA_Q_RANK = 1024
MLA_KV_RANK = 512
MLA_NOPE = 128
MLA_ROPE = 64
MLA_V = 128
MLA_QK = MLA_NOPE + MLA_ROPE
MLA_IN_DIM = MLA_Q_RANK + MLA_KV_RANK + MLA_ROPE
ROPE_THETA = 10000.0
Q_BLOCK = 128

kernel_name = 'hybrid_gdn_mla_macaron_step'


def rmsnorm(x, g):
    xf = x.astype(jnp.float32)
    y = xf * lax.rsqrt(jnp.mean(xf * xf, axis=-1, keepdims=True) + EPS)
    return (y * g.astype(jnp.float32)).astype(x.dtype)


def l2norm(x):
    xf = x.astype(jnp.float32)
    return xf * lax.rsqrt(jnp.sum(xf * xf, axis=-1, keepdims=True) + EPS)


def macaron_half(x, g, w_gate, w_up, w_down):
    h = rmsnorm(x, g)
    return x + 0.5 * ((jax.nn.silu(h @ w_gate) * (h @ w_up)) @ w_down)


def rope(x, pos):
    half = MLA_ROPE // 2
    inv = ROPE_THETA ** (-jnp.arange(half, dtype=jnp.float32) / half)
    ang = pos.astype(jnp.float32)[:, None] * inv[None, :]
    ang = ang.reshape(ang.shape[:1] + (1,) * (x.ndim - 3) + ang.shape[1:])
    cos, sin = jnp.cos(ang), jnp.sin(ang)
    x1, x2 = x[..., :half], x[..., half:]
    return jnp.concatenate([x1 * cos - x2 * sin, x2 * cos + x1 * sin], axis=-1).astype(x.dtype)


def causal_short_conv(x_pad, w):
    t = x_pad.shape[1] - (GDN_CONV - 1)
    out = x_pad[:, 0:t] * w[0]
    for j in range(1, GDN_CONV):
        out = out + x_pad[:, j:j + t] * w[j]
    return out


def gated_delta_rule(q, k, v, g, beta, s0):
    f32 = jnp.float32
    b, t, h, dk = q.shape
    dv = v.shape[-1]
    c = GDN_CHUNK if t % GDN_CHUNK == 0 else t
    n = t // c

    def blk4(x):
        return x.astype(f32).reshape(b, n, c, h, x.shape[-1]).transpose(1, 0, 3, 2, 4)

    def blk3(x):
        return x.astype(f32).reshape(b, n, c, h).transpose(1, 0, 3, 2)

    qb = blk4(q) * dk ** -0.5
    kb, vb = blk4(k), blk4(v)
    gb, bb = blk3(g), blk3(beta)
    gc = jnp.cumsum(gb, axis=-1)
    tri = jnp.tril(jnp.ones((c, c), bool))
    strict = jnp.tril(jnp.ones((c, c), bool), -1)
    decay = jnp.exp(jnp.where(tri, gc[..., :, None] - gc[..., None, :], -jnp.inf))
    k_beta = kb * bb[..., None]
    lower = jnp.where(strict, jnp.einsum('nbhid,nbhjd->nbhij', k_beta, kb) * decay, 0.0)
    a_mat = lower + jnp.eye(c, dtype=f32)
    rhs = jnp.concatenate([vb * bb[..., None], k_beta * jnp.exp(gc)[..., None]], axis=-1)
    sol = lax.linalg.triangular_solve(a_mat, rhs, left_side=True, lower=True, unit_diagonal=True)
    u, w = sol[..., :dv], sol[..., dv:]
    qk = jnp.einsum('nbhid,nbhjd->nbhij', qb, kb) * decay
    q_dec = qb * jnp.exp(gc)[..., None]
    k_dec = kb * jnp.exp(gc[..., -1:] - gc)[..., None]
    g_last = jnp.exp(gc[..., -1])

    def step(s, xs):
        u_i, w_i, qk_i, q_i, k_i, gl_i = xs
        v_new = u_i - jnp.einsum('bhcd,bhde->bhce', w_i, s)
        o_i = jnp.einsum('bhcd,bhde->bhce', q_i, s) + jnp.einsum('bhij,bhje->bhie', qk_i, v_new)
        s = s * gl_i[..., None, None] + jnp.einsum('bhcd,bhce->bhde', k_i, v_new)
        return s, o_i

    s_fin, o = lax.scan(step, s0.astype(f32), (u, w, qk, q_dec, k_dec, g_last))
    o = o.transpose(1, 0, 3, 2, 4).reshape(b, t, h, dv)
    return o, s_fin


def gdn_mixer(h, conv_buf, s0, w_in, conv_w, a_log, dt_bias, norm_g, w_out):
    b, t, _ = h.shape
    qkv, z, beta_in, a_in = jnp.split(
        h @ w_in, [GDN_CONV_DIM, GDN_CONV_DIM + GDN_V_DIM, GDN_CONV_DIM + GDN_V_DIM + GDN_V_HEADS], axis=-1)
    qkv_pad = jnp.concatenate([conv_buf.astype(qkv.dtype), qkv], axis=1)
    new_buf = qkv_pad[:, qkv_pad.shape[1] - (GDN_CONV - 1):]
    qkv = jax.nn.silu(causal_short_conv(qkv_pad, conv_w))
    q, k, v = jnp.split(qkv, [GDN_QK_DIM, 2 * GDN_QK_DIM], axis=-1)
    rep = GDN_V_HEADS // GDN_K_HEADS
    q = jnp.repeat(l2norm(q.reshape(b, t, GDN_K_HEADS, GDN_HEAD_DK)), rep, axis=2)
    k = jnp.repeat(l2norm(k.reshape(b, t, GDN_K_HEADS, GDN_HEAD_DK)), rep, axis=2)
    v = v.reshape(b, t, GDN_V_HEADS, GDN_HEAD_DV)
    g = -jnp.exp(a_log.astype(jnp.float32)) * jax.nn.softplus(a_in.astype(jnp.float32) + dt_bias.astype(jnp.float32))
    beta = jax.nn.sigmoid(beta_in.astype(jnp.float32))
    o, s_new = gated_delta_rule(q, k, v, g, beta, s0)
    o = rmsnorm(o.astype(h.dtype), norm_g) * jax.nn.silu(z.reshape(b, t, GDN_V_HEADS, GDN_HEAD_DV))
    return o.reshape(b, t, GDN_V_DIM) @ w_out, new_buf, s_new


def mla_qkv(h, pos, w_in, q_norm_g, kv_norm_g, w_uq, w_uk, qk_g_q, qk_g_k):
    b, t, _ = h.shape
    cq, ckv, k_r = jnp.split(h @ w_in, [MLA_Q_RANK, MLA_Q_RANK + MLA_KV_RANK], axis=-1)
    q = rmsnorm((rmsnorm(cq, q_norm_g) @ w_uq).reshape(b, t, MLA_HEADS, MLA_QK), qk_g_q)
    q_nope, q_rope = q[..., :MLA_NOPE], rope(q[..., MLA_NOPE:], pos)
    lat = rmsnorm(ckv, kv_norm_g)
    k_nope = jnp.einsum('btc,chn->bthn', lat, w_uk)
    kf, krf = k_nope.astype(jnp.float32), k_r.astype(jnp.float32)
    ms = (jnp.sum(kf * kf, axis=-1) + jnp.sum(krf * krf, axis=-1)[..., None]) / MLA_QK
    rinv = lax.rsqrt(ms + EPS)
    k_pe = rope(k_r * qk_g_k[MLA_NOPE:], pos)
    return q_nope, q_rope, lat, k_nope, k_pe, rinv.astype(h.dtype)


def mla_prompt_attention(q, k, v):
    b, t, h, _ = q.shape
    nq = t // Q_BLOCK
    scale = MLA_QK ** -0.5
    qb = q.reshape(b, nq, Q_BLOCK, h, MLA_QK).transpose(1, 0, 2, 3, 4)
    kpos = jnp.arange(t)

    def one_block(args):
        i, q_i = args
        s = jnp.einsum('bqhd,bkhd->bhqk', q_i, k, preferred_element_type=jnp.float32) * scale
        qpos = i * Q_BLOCK + jnp.arange(Q_BLOCK)
        s = jnp.where(qpos[:, None] >= kpos[None, :], s, -jnp.inf)
        p = jax.nn.softmax(s, axis=-1)
        return jnp.einsum('bhqk,bkhe->bqhe', p.astype(v.dtype), v)

    o = lax.map(one_block, (jnp.arange(nq), qb))
    return o.transpose(1, 0, 2, 3, 4).reshape(b, t, h, MLA_V)


def mla_sample_attention(q_lat, q_pe, lat_new, pe_new, rinv_new, cache_latent, cache_k_pe, cache_k_rinv,
                         layer_idx, page_table):
    f32 = jnp.float32
    b, s, h, _ = q_lat.shape
    scale = MLA_QK ** -0.5

    def scores(lat, pe, rinv):
        sc = jnp.einsum('bshc,bpc->bshp', q_lat, lat, preferred_element_type=f32)
        sc = sc + jnp.einsum('bshr,bpr->bshp', q_pe, pe, preferred_element_type=f32)
        return sc * (scale * jnp.swapaxes(rinv, 1, 2).astype(f32))[:, None]

    def accumulate(carry, sc, lat):
        m, l, acc = carry
        m_new = jnp.maximum(m, jnp.max(sc, axis=-1))
        corr = jnp.exp(m - m_new)
        p = jnp.exp(sc - m_new[..., None])
        l = l * corr + jnp.sum(p, axis=-1)
        acc = acc * corr[..., None] + jnp.einsum('bshp,bpc->bshc', p, lat.astype(f32))
        return (m_new, l, acc)

    def page_step(carry, pages):
        lat = cache_latent[layer_idx, pages]
        sc = scores(lat, cache_k_pe[layer_idx, pages], cache_k_rinv[layer_idx, pages])
        return accumulate(carry, sc, lat), None

    init = (jnp.full((b, s, h), -jnp.inf, f32), jnp.zeros((b, s, h), f32), jnp.zeros((b, s, h, MLA_KV_RANK), f32))
    carry, _ = lax.scan(page_step, init, page_table.T)
    causal = jnp.tril(jnp.ones((s, s), bool))
    sc_new = jnp.where(causal[None, :, None, :], scores(lat_new, pe_new, rinv_new), -jnp.inf)
    m, l, acc = accumulate(carry, sc_new, lat_new)
    return acc / l[..., None]


def setup_inputs(seed: int = 0) -> dict:
    key = jax.random.key(seed)
    ks = jax.random.split(key, 32)
    f32 = jnp.float32
    n_pages = PAST_LEN // PAGE_SIZE
    n_phys = (DEC_BATCH * n_pages * 5) // 4

    def w(k, shape, fan_in):
        return jax.random.normal(k, shape, f32) * fan_in ** -0.5

    def gain(k, shape):
        return 1.0 + 0.1 * jax.random.normal(k, shape, f32)

    page_table = jax.random.permutation(ks[7], n_phys)[: DEC_BATCH * n_pages].reshape(DEC_BATCH, n_pages).astype(jnp.int32)
    dt = jnp.exp(jax.random.uniform(ks[15], (N_GDN, GDN_V_HEADS), f32, math.log(1e-3), math.log(1e-1)))
    return {
        'x_prompt': jax.random.normal(ks[0], (BATCH, SEQ, D_MODEL), f32),
        'x_sample': jax.random.normal(ks[1], (DEC_BATCH, DEC_SEQ, D_MODEL), f32),
        'state_gdn_ssm': 0.1 * jax.random.normal(ks[2], (N_GDN, DEC_BATCH, GDN_V_HEADS, GDN_HEAD_DK, GDN_HEAD_DV), f32),
        'state_gdn_conv': jax.random.normal(ks[3], (N_GDN, DEC_BATCH, GDN_CONV - 1, GDN_CONV_DIM), f32),
        'cache_mla_latent': jax.random.normal(ks[4], (N_MLA, n_phys, PAGE_SIZE, MLA_KV_RANK), f32),
        'cache_mla_k_pe': jax.random.normal(ks[5], (N_MLA, n_phys, PAGE_SIZE, MLA_ROPE), f32),
        'cache_mla_k_rinv': jax.random.uniform(ks[6], (N_MLA, n_phys, PAGE_SIZE, MLA_HEADS), f32, 0.8, 1.2),
        'page_table': page_table,
        'norm_g': gain(ks[8], (DEPTH, 4, D_MODEL)),
        'ffn_w_gate': w(ks[9], (DEPTH, 2, D_MODEL, D_FF), D_MODEL),
        'ffn_w_up': w(ks[10], (DEPTH, 2, D_MODEL, D_FF), D_MODEL),
        'ffn_w_down': w(ks[11], (DEPTH, 2, D_FF, D_MODEL), D_FF),
        'gdn_w_in': w(ks[12], (N_GDN, D_MODEL, GDN_IN_DIM), D_MODEL),
        'gdn_conv_w': w(ks[13], (N_GDN, GDN_CONV, GDN_CONV_DIM), GDN_CONV),
        'gdn_a_log': jnp.log(jax.random.uniform(ks[14], (N_GDN, GDN_V_HEADS), f32, 1.0, 16.0)),
        'gdn_dt_bias': dt + jnp.log(-jnp.expm1(-dt)),
        'gdn_norm_g': gain(ks[16], (N_GDN, GDN_HEAD_DV)),
        'gdn_w_out': w(ks[17], (N_GDN, GDN_V_DIM, D_MODEL), GDN_V_DIM),
        'mla_w_in': w(ks[18], (N_MLA, D_MODEL, MLA_IN_DIM), D_MODEL),
        'mla_q_norm_g': gain(ks[19], (N_MLA, MLA_Q_RANK)),
        'mla_kv_norm_g': gain(ks[20], (N_MLA, MLA_KV_RANK)),
        'mla_w_uq': w(ks[21], (N_MLA, MLA_Q_RANK, MLA_HEADS * MLA_QK), MLA_Q_RANK),
        'mla_w_uk': w(ks[22], (N_MLA, MLA_KV_RANK, MLA_HEADS, MLA_NOPE), MLA_KV_RANK),
        'mla_w_uv': w(ks[23], (N_MLA, MLA_KV_RANK, MLA_HEADS, MLA_V), MLA_KV_RANK),
        'mla_qk_norm_q': gain(ks[24], (N_MLA, MLA_QK)),
        'mla_qk_norm_k': gain(ks[25], (N_MLA, MLA_QK)),
        'mla_w_o': w(ks[26], (N_MLA, MLA_HEADS * MLA_V, D_MODEL), MLA_HEADS * MLA_V),
    }


def reference(x_prompt, x_sample, state_gdn_ssm, state_gdn_conv, cache_mla_latent, cache_mla_k_pe,
              cache_mla_k_rinv, page_table, norm_g, ffn_w_gate, ffn_w_up, ffn_w_down, gdn_w_in, gdn_conv_w,
              gdn_a_log, gdn_dt_bias, gdn_norm_g, gdn_w_out, mla_w_in, mla_q_norm_g, mla_kv_norm_g, mla_w_uq,
              mla_w_uk, mla_w_uv, mla_qk_norm_q, mla_qk_norm_k, mla_w_o):
    pos_p = jnp.arange(SEQ)
    pos_s = PAST_LEN + jnp.arange(DEC_SEQ)
    xp, xs = x_prompt, x_sample
    ssm_p, conv_p, ssm_s, conv_s = [], [], [], []
    lat_p, pe_p, rinv_p, lat_s, pe_s, rinv_s = [], [], [], [], [], []
    for layer in range(DEPTH):
        i = layer // N_MIXERS
        ffn_a = (norm_g[layer, 0], ffn_w_gate[layer, 0], ffn_w_up[layer, 0], ffn_w_down[layer, 0])
        ffn_b = (norm_g[layer, 2], ffn_w_gate[layer, 1], ffn_w_up[layer, 1], ffn_w_down[layer, 1])
        xp = macaron_half(xp, *ffn_a)
        xs = macaron_half(xs, *ffn_a)
        hp, hs = rmsnorm(xp, norm_g[layer, 1]), rmsnorm(xs, norm_g[layer, 1])
        if layer % N_MIXERS == 0:
            gw = (gdn_w_in[i], gdn_conv_w[i], gdn_a_log[i], gdn_dt_bias[i], gdn_norm_g[i], gdn_w_out[i])
            conv0 = jnp.zeros((BATCH, GDN_CONV - 1, GDN_CONV_DIM), hp.dtype)
            ssm0 = jnp.zeros((BATCH, GDN_V_HEADS, GDN_HEAD_DK, GDN_HEAD_DV), jnp.float32)
            mp, cb_p, s_p = gdn_mixer(hp, conv0, ssm0, *gw)
            ms, cb_s, s_s = gdn_mixer(hs, state_gdn_conv[i], state_gdn_ssm[i], *gw)
            ssm_p.append(s_p.astype(state_gdn_ssm.dtype))
            conv_p.append(cb_p.astype(state_gdn_conv.dtype))
            ssm_s.append(s_s.astype(state_gdn_ssm.dtype))
            conv_s.append(cb_s.astype(state_gdn_conv.dtype))
        else:
            mw = (mla_w_in[i], mla_q_norm_g[i], mla_kv_norm_g[i], mla_w_uq[i], mla_w_uk[i],
                  mla_qk_norm_q[i], mla_qk_norm_k[i])
            g_k_nope = mla_qk_norm_k[i, :MLA_NOPE]
            q_nope, q_rope, lat, k_nope, k_pe, rinv = mla_qkv(hp, pos_p, *mw)
            k_full = jnp.concatenate(
                [k_nope * g_k_nope, jnp.broadcast_to(k_pe[:, :, None, :], k_nope.shape[:3] + (MLA_ROPE,))], axis=-1)
            k_full = (k_full * rinv[..., None]).astype(hp.dtype)
            v_full = jnp.einsum('btc,chv->bthv', lat, mla_w_uv[i])
            o = mla_prompt_attention(jnp.concatenate([q_nope, q_rope], axis=-1), k_full, v_full)
            mp = o.reshape(BATCH, SEQ, MLA_HEADS * MLA_V) @ mla_w_o[i]
            lat_p.append(lat.astype(cache_mla_latent.dtype))
            pe_p.append(k_pe.astype(cache_mla_k_pe.dtype))
            rinv_p.append(rinv.astype(cache_mla_k_rinv.dtype))
            q_nope, q_rope, lat, k_nope, k_pe, rinv = mla_qkv(hs, pos_s, *mw)
            q_lat = jnp.einsum('bshn,chn->bshc', q_nope * g_k_nope, mla_w_uk[i])
            o_lat = mla_sample_attention(q_lat, q_rope, lat, k_pe, rinv, cache_mla_latent, cache_mla_k_pe,
                                         cache_mla_k_rinv, i, page_table)
            o = jnp.einsum('bshc,chv->bshv', o_lat.astype(hs.dtype), mla_w_uv[i])
            ms = o.reshape(DEC_BATCH, DEC_SEQ, MLA_HEADS * MLA_V) @ mla_w_o[i]
            lat_s.append(lat.astype(cache_mla_latent.dtype))
            pe_s.append(k_pe.astype(cache_mla_k_pe.dtype))
            rinv_s.append(rinv.astype(cache_mla_k_rinv.dtype))
        xp = xp + mp
        xs = xs + ms
        xp = rmsnorm(macaron_half(xp, *ffn_b), norm_g[layer, 3])
        xs = rmsnorm(macaron_half(xs, *ffn_b), norm_g[layer, 3])
    return (xp, xs,
            jnp.stack(ssm_p), jnp.stack(conv_p), jnp.stack(lat_p), jnp.stack(pe_p), jnp.stack(rinv_p),
            jnp.stack(ssm_s), jnp.stack(conv_s), jnp.stack(lat_s), jnp.stack(pe_s), jnp.stack(rinv_s))
```

```python
import functools
import math

import jax
import jax.numpy as jnp
from jax import lax
from jax.experimental import pallas as pl
from jax.experimental.pallas import tpu as pltpu

F32 = jnp.float32
BF16 = jnp.bfloat16
EPS = 1e-6
ROPE_THETA = 10000.0

GDN_K_HEADS = 16
GDN_V_HEADS = 32
GDN_HEAD = 128
GDN_CONV = 4
GDN_CHUNK = 64
MLA_HEADS = 64
MLA_Q_RANK = 1024
MLA_KV_RANK = 512
MLA_NOPE = 128
MLA_ROPE = 64
MLA_V = 128
MLA_QK = MLA_NOPE + MLA_ROPE

LANE = 128
SUBLANE_BF16 = 16
VMEM_LIMIT = 56 * 1024 * 1024

ROW_TILE = 1088
GDN_HEADS_PER_STEP = 4
MLA_Q_HEADS_PER_STEP = 8
MLA_KV_HEADS_PER_STEP = 8
FLASH_HEADS_PER_STEP = 2
FLASH_BLOCK = 256
PAGES_PER_STEP = 8
SAMPLE_PAD = 16


def _tile(n, target, align):
    if n <= target:
        return n
    t = (target // align) * align
    while t >= align:
        if n % t == 0:
            return t
        t -= align
    return n


def _cparams(sem):
    return pltpu.CompilerParams(dimension_semantics=sem, vmem_limit_bytes=VMEM_LIMIT)


def _dot(a, b):
    return jnp.dot(a.astype(BF16), b.astype(BF16), preferred_element_type=F32)


def _dot_nt(a, b):
    return lax.dot_general(a.astype(BF16), b.astype(BF16), (((1,), (1,)), ((), ())),
                           preferred_element_type=F32)


def _dot_tn(a, b):
    return lax.dot_general(a.astype(BF16), b.astype(BF16), (((0,), (0,)), ((), ())),
                           preferred_element_type=F32)


def _split3(x):
    x1 = x.astype(BF16)
    r1 = x - x1.astype(F32)
    x2 = r1.astype(BF16)
    x3 = (r1 - x2.astype(F32)).astype(BF16)
    return x1, x2, x3


def _rms(x, g):
    return x * lax.rsqrt(jnp.mean(x * x, axis=-1, keepdims=True) + EPS) * g


def _sigmoid(x):
    return 1.0 / (1.0 + jnp.exp(-x))


def _norm_kernel(x_ref, g_ref, o_ref):
    o_ref[...] = _rms(x_ref[...], g_ref[...]).astype(o_ref.dtype)


def _norm2_kernel(x_ref, g1_ref, g2_ref, y_ref, h_ref):
    y = _rms(x_ref[...], g1_ref[...])
    y_ref[...] = y
    h_ref[...] = _rms(y, g2_ref[...]).astype(h_ref.dtype)


def _norm(x, g, out_dtype):
    m, d = x.shape
    tm = _tile(m, 512, SUBLANE_BF16)
    return pl.pallas_call(
        _norm_kernel,
        grid=(m // tm,),
        in_specs=[pl.BlockSpec((tm, d), lambda i: (i, 0)), pl.BlockSpec((1, d), lambda i: (0, 0))],
        out_specs=pl.BlockSpec((tm, d), lambda i: (i, 0)),
        out_shape=jax.ShapeDtypeStruct((m, d), out_dtype),
        compiler_params=_cparams(("parallel",)),
        name="rmsnorm",
    )(x, g.reshape(1, d))


def _norm2(x, g1, g2):
    m, d = x.shape
    tm = _tile(m, 512, SUBLANE_BF16)
    return pl.pallas_call(
        _norm2_kernel,
        grid=(m // tm,),
        in_specs=[pl.BlockSpec((tm, d), lambda i: (i, 0)), pl.BlockSpec((1, d), lambda i: (0, 0)),
                  pl.BlockSpec((1, d), lambda i: (0, 0))],
        out_specs=[pl.BlockSpec((tm, d), lambda i: (i, 0)), pl.BlockSpec((tm, d), lambda i: (i, 0))],
        out_shape=[jax.ShapeDtypeStruct((m, d), F32), jax.ShapeDtypeStruct((m, d), BF16)],
        compiler_params=_cparams(("parallel",)),
        name="rmsnorm_pair",
    )(x, g1.reshape(1, d), g2.reshape(1, d))


def _mm_kernel(*refs, has_res, nk):
    if has_res:
        a_ref, w_ref, r_ref, o_ref = refs[:4]
    else:
        a_ref, w_ref, o_ref = refs[:3]
        r_ref = None
    part = _dot(a_ref[...], w_ref[...])
    if nk == 1:
        if has_res:
            part = part + r_ref[...]
        o_ref[...] = part.astype(o_ref.dtype)
        return
    acc_ref = refs[-1]
    k = pl.program_id(2)

    @pl.when(k == 0)
    def _():
        acc_ref[...] = part

    @pl.when(k > 0)
    def _():
        acc_ref[...] += part

    @pl.when(k == nk - 1)
    def _():
        res = acc_ref[...]
        if has_res:
            res = res + r_ref[...]
        o_ref[...] = res.astype(o_ref.dtype)


def _mm(a, w, wprefix=(), *, n_out=None, out_dtype=F32, residual=None, tm=ROW_TILE, tn=1024, tk=1024,
        name="matmul"):
    m, kdim = a.shape
    assert w.shape[-2] == kdim
    n = n_out or w.shape[-1]
    tm = _tile(m, tm, SUBLANE_BF16)
    tn = _tile(n, tn, LANE)
    tk = _tile(kdim, tk, LANE)
    nk = kdim // tk
    npre = len(wprefix)
    in_specs = [pl.BlockSpec((tm, tk), lambda i, j, k: (i, k)),
                pl.BlockSpec((None,) * npre + (tk, tn), lambda i, j, k: tuple(wprefix) + (k, j))]
    args = [a, w]
    if residual is not None:
        in_specs.append(pl.BlockSpec((tm, tn), lambda i, j, k: (i, j)))
        args.append(residual)
    return pl.pallas_call(
        functools.partial(_mm_kernel, has_res=residual is not None, nk=nk),
        grid=(m // tm, n // tn, nk),
        in_specs=in_specs,
        out_specs=pl.BlockSpec((tm, tn), lambda i, j, k: (i, j)),
        out_shape=jax.ShapeDtypeStruct((m, n), out_dtype),
        scratch_shapes=[pltpu.VMEM((tm, tn), F32)] if nk > 1 else [],
        compiler_params=_cparams(("parallel", "parallel", "arbitrary")),
        name=name,
    )(*args)


def _ffn_up_kernel(h_ref, wg_ref, wu_ref, o_ref, *, n_valid):
    f = pl.program_id(1)

    @pl.when(f < n_valid)
    def _():
        h = h_ref[...]
        gate = _dot(h, wg_ref[...])
        up = _dot(h, wu_ref[...])
        o_ref[...] = (0.5 * gate * _sigmoid(gate) * up).astype(o_ref.dtype)

    @pl.when(f >= n_valid)
    def _():
        o_ref[...] = jnp.zeros_like(o_ref)


def _ffn_up(h, w_gate, w_up, layer, half, f_pad):
    m, d = h.shape
    f_dim = w_gate.shape[-1]
    tm = _tile(m, ROW_TILE, SUBLANE_BF16)
    tf = _tile(f_dim, 256, LANE)
    assert f_pad % tf == 0
    n_valid = f_dim // tf

    def wmap(i, f):
        return (layer, half, 0, jnp.minimum(f, n_valid - 1))

    return pl.pallas_call(
        functools.partial(_ffn_up_kernel, n_valid=n_valid),
        grid=(m // tm, f_pad // tf),
        in_specs=[pl.BlockSpec((tm, d), lambda i, f: (i, 0)),
                  pl.BlockSpec((None, None, d, tf), wmap),
                  pl.BlockSpec((None, None, d, tf), wmap)],
        out_specs=pl.BlockSpec((tm, tf), lambda i, f: (i, f)),
        out_shape=jax.ShapeDtypeStruct((m, f_pad), BF16),
        compiler_params=_cparams(("parallel", "arbitrary")),
        name="ffn_up",
    )(h, w_gate, w_up)


def _macaron_half(x, h, w_gate, w_up, w_down, layer, half):
    f_dim = w_gate.shape[-1]
    f_pad = -(-f_dim // 1024) * 1024 if f_dim > 1024 else f_dim
    act = _ffn_up(h, w_gate, w_up, layer, half, f_pad)
    wd = jnp.pad(w_down[layer, half].astype(BF16), ((0, f_pad - f_dim), (0, 0)))
    return _mm(act, wd, residual=x, name="ffn_down")


def _gates_kernel(b_ref, a_ref, alog_ref, dt_ref, beta_ref, g_ref):
    beta_ref[...] = _sigmoid(b_ref[...])
    x = a_ref[...] + dt_ref[...]
    softplus = jnp.maximum(x, 0.0) + jnp.log1p(jnp.exp(-jnp.abs(x)))
    g_ref[...] = -jnp.exp(alog_ref[...]) * softplus


def _gdn_gates(beta_in, a_in, a_log, dt_bias):
    m, h = beta_in.shape
    tm = _tile(m, 2176, 8)
    row = pl.BlockSpec((tm, h), lambda i: (i, 0))
    vec = pl.BlockSpec((1, h), lambda i: (0, 0))
    return pl.pallas_call(
        _gates_kernel,
        grid=(m // tm,),
        in_specs=[row, row, vec, vec],
        out_specs=[row, row],
        out_shape=[jax.ShapeDtypeStruct((m, h), F32)] * 2,
        compiler_params=_cparams(("parallel",)),
        name="gdn_gates",
    )(beta_in, a_in, a_log.reshape(1, h), dt_bias.reshape(1, h))


def _conv_finish(y, o_ref, row_sl, is_qk, qk_scale):
    y = y * _sigmoid(y)
    tc = y.shape[-1]

    @pl.when(is_qk)
    def _():
        for h in range(tc // GDN_HEAD):
            sl = slice(h * GDN_HEAD, (h + 1) * GDN_HEAD)
            yh = y[:, sl]
            r = lax.rsqrt(jnp.sum(yh * yh, axis=-1, keepdims=True) + EPS)
            o_ref[row_sl + (sl,)] = (yh * (r * qk_scale)).astype(o_ref.dtype)

    @pl.when(jnp.logical_not(is_qk))
    def _():
        o_ref[row_sl + (slice(None),)] = y.astype(o_ref.dtype)


def _conv_prompt_kernel(x_ref, halo_ref, w_ref, o_ref, *, tiles_per_seq, nq_tiles, nqk_tiles):
    i = pl.program_id(0)
    c = pl.program_id(1)
    x = x_ref[...]
    tt = x.shape[0]
    halo = jnp.where(i % tiles_per_seq == 0, 0.0, halo_ref[...])
    xp = jnp.concatenate([halo, x], axis=0)
    w = w_ref[...]
    hl = halo.shape[0]
    y = xp[hl - 3:hl - 3 + tt] * w[0:1]
    for j in range(1, GDN_CONV):
        y = y + xp[hl - 3 + j:hl - 3 + j + tt] * w[j:j + 1]
    qk_scale = jnp.where(c < nq_tiles, GDN_HEAD ** -0.5, 1.0)
    _conv_finish(y, o_ref, (slice(None),), c < nqk_tiles, qk_scale)


def _conv_sample_kernel(x_ref, w_ref, o_ref, *, steps, nq_tiles, nqk_tiles):
    c = pl.program_id(0)
    w = w_ref[...]
    qk_scale = jnp.where(c < nq_tiles, GDN_HEAD ** -0.5, 1.0)
    for s in range(steps):
        y = x_ref[s] * w[0:1]
        for j in range(1, GDN_CONV):
            y = y + x_ref[s + j] * w[j:j + 1]
        _conv_finish(y, o_ref, (s, slice(None)), c < nqk_tiles, qk_scale)


def _gdn_conv_prompt(qkvz, conv_w, mp, seq):
    cdim = conv_w.shape[-1]
    tt = _tile(seq, 256, 8)
    tc = _tile(2 * GDN_K_HEADS * GDN_HEAD // 2, 512, GDN_HEAD)
    nq = GDN_K_HEADS * GDN_HEAD // tc
    return pl.pallas_call(
        functools.partial(_conv_prompt_kernel, tiles_per_seq=seq // tt, nq_tiles=nq, nqk_tiles=2 * nq),
        grid=(mp // tt, cdim // tc),
        in_specs=[pl.BlockSpec((tt, tc), lambda i, c: (i, c)),
                  pl.BlockSpec((8, tc), lambda i, c: (jnp.maximum(i * (tt // 8) - 1, 0), c)),
                  pl.BlockSpec((GDN_CONV, tc), lambda i, c: (0, c))],
        out_specs=pl.BlockSpec((tt, tc), lambda i, c: (i, c)),
        out_shape=jax.ShapeDtypeStruct((mp, cdim), BF16),
        compiler_params=_cparams(("parallel", "parallel")),
        name="gdn_conv_prompt",
    )(qkvz, qkvz, conv_w)


def _gdn_conv_sample(xpad_t, conv_w, steps):
    tp, nb, cdim = xpad_t.shape
    tc = _tile(GDN_K_HEADS * GDN_HEAD, 512, GDN_HEAD)
    nq = GDN_K_HEADS * GDN_HEAD // tc
    return pl.pallas_call(
        functools.partial(_conv_sample_kernel, steps=steps, nq_tiles=nq, nqk_tiles=2 * nq),
        grid=(cdim // tc,),
        in_specs=[pl.BlockSpec((tp, nb, tc), lambda c: (0, 0, c)),
                  pl.BlockSpec((GDN_CONV, tc), lambda c: (0, c))],
        out_specs=pl.BlockSpec((steps, nb, tc), lambda c: (0, 0, c)),
        out_shape=jax.ShapeDtypeStruct((steps, nb, cdim), BF16),
        compiler_params=_cparams(("parallel",)),
        name="gdn_conv_sample",
    )(xpad_t, conv_w)


def _gdn_chunk_kernel(*refs, c, hg, rep, nchunks, has_s0):
    if has_s0:
        q_ref, k_ref, v_ref, z_ref, g_ref, b_ref, ng_ref, s0_ref, o_ref, sfin_ref, s_scr = refs
    else:
        q_ref, k_ref, v_ref, z_ref, g_ref, b_ref, ng_ref, o_ref, sfin_ref, s_scr = refs
    n = pl.program_id(2)

    @pl.when(n == 0)
    def _():
        s_scr[...] = s0_ref[...] if has_s0 else jnp.zeros_like(s_scr)

    d = GDN_HEAD
    row = lax.broadcasted_iota(jnp.int32, (c, c), 0)
    col = lax.broadcasted_iota(jnp.int32, (c, c), 1)
    tril = row >= col
    strict = row > col
    eye = row == col
    eye_f = eye.astype(F32)
    ng = ng_ref[...]
    for j in range(hg):
        hs = slice(j * d, (j + 1) * d)
        ks = slice((j // rep) * d, (j // rep + 1) * d)
        q = q_ref[:, ks]
        k = k_ref[:, ks]
        kf = k.astype(F32)
        vf = v_ref[:, hs].astype(F32)
        g = g_ref[:, j:j + 1]
        beta = b_ref[:, j:j + 1]
        g_row = jnp.sum(jnp.where(eye, g, 0.0), axis=0, keepdims=True)
        gc_col = jnp.sum(jnp.where(tril, g_row, 0.0), axis=1, keepdims=True)
        gc_row = jnp.sum(jnp.where(row <= col, g, 0.0), axis=0, keepdims=True)
        decay = jnp.where(tril, jnp.exp(jnp.where(tril, gc_col - gc_row, 0.0)), 0.0)
        qk_kk = _dot_nt(jnp.concatenate([q, k], axis=0), k)
        qk = qk_kk[:c] * decay
        low = jnp.where(strict, qk_kk[c:] * beta * decay, 0.0)
        mm = -low
        inv = eye_f + mm
        for _ in range(int(math.log2(c)) - 1):
            mm = _dot(mm, mm)
            inv = inv + _dot(inv, mm)
        e_col = jnp.exp(gc_col)
        uw = _dot(inv, jnp.concatenate([vf * beta, kf * (beta * e_col)], axis=1))
        s_old = s_scr[j]
        wq = _dot(jnp.concatenate([uw[:, d:], q.astype(F32) * e_col], axis=0), s_old)
        v_new = uw[:, :d] - wq[:c]
        o = wq[c:] + _dot(qk, v_new)
        gc_last = gc_col[c - 1:c, :]
        k_dec = kf * jnp.exp(gc_last - gc_col)
        s_scr[j] = s_old * jnp.exp(gc_last) + _dot_tn(k_dec, v_new)
        z = z_ref[:, hs]
        o_ref[:, hs] = (_rms(o, ng) * (z * _sigmoid(z))).astype(o_ref.dtype)

    @pl.when(n == nchunks - 1)
    def _():
        sfin_ref[...] = s_scr[...]


def _gdn_rule(qkv, z, z_col0, g, beta, norm_g, s0, nb, seq, c):
    d = GDN_HEAD
    hg = GDN_HEADS_PER_STEP
    rep = GDN_V_HEADS // GDN_K_HEADS
    nchunks = seq // c
    ngrp = GDN_V_HEADS // hg
    koff = GDN_K_HEADS // (hg // rep)
    voff = 2 * GDN_K_HEADS // hg
    zoff = z_col0 // (hg * d)
    rowmap = lambda b, h, n: (b * nchunks + n, h)
    in_specs = [pl.BlockSpec((c, hg // rep * d), rowmap),
                pl.BlockSpec((c, hg // rep * d), lambda b, h, n: (b * nchunks + n, koff + h)),
                pl.BlockSpec((c, hg * d), lambda b, h, n: (b * nchunks + n, voff + h)),
                pl.BlockSpec((c, hg * d), lambda b, h, n: (b * nchunks + n, zoff + h)),
                pl.BlockSpec((None, c, hg), lambda b, h, n: (h, b * nchunks + n, 0)),
                pl.BlockSpec((None, c, hg), lambda b, h, n: (h, b * nchunks + n, 0)),
                pl.BlockSpec((1, d), lambda b, h, n: (0, 0))]
    args = [qkv, qkv, qkv, z, g, beta, norm_g.reshape(1, d)]
    if s0 is not None:
        in_specs.append(pl.BlockSpec((None, hg, d, d), lambda b, h, n: (b, h, 0, 0)))
        args.append(s0)
    return pl.pallas_call(
        functools.partial(_gdn_chunk_kernel, c=c, hg=hg, rep=rep, nchunks=nchunks, has_s0=s0 is not None),
        grid=(nb, ngrp, nchunks),
        in_specs=in_specs,
        out_specs=[pl.BlockSpec((c, hg * d), rowmap),
                   pl.BlockSpec((None, hg, d, d), lambda b, h, n: (b, h, 0, 0))],
        out_shape=[jax.ShapeDtypeStruct((nb * seq, GDN_V_HEADS * d), BF16),
                   jax.ShapeDtypeStruct((nb, GDN_V_HEADS, d, d), F32)],
        scratch_shapes=[pltpu.VMEM((hg, d, d), F32)],
        compiler_params=_cparams(("parallel", "parallel", "arbitrary")),
        name="gdn_rule",
    )(*args)


def _head_groups(x, hg):
    r, h = x.shape
    return x.reshape(r, h // hg, hg).transpose(1, 0, 2)


def _pad_steps(x, nb, steps, pad_to):
    cdim = x.shape[-1]
    x = x.reshape(nb, steps, cdim)
    return jnp.pad(x, ((0, 0), (0, pad_to - steps), (0, 0))).reshape(nb * pad_to, cdim)


def _gdn_mixer(h, x_res, state_ssm, state_conv, w_in, conv_w, a_log, dt_bias, norm_g, w_out, li, dims):
    mp, ms, nbp, seq, nbs, steps = dims
    cdim = conv_w.shape[-1]
    vdim = GDN_V_HEADS * GDN_HEAD
    qk = GDN_K_HEADS * GDN_HEAD
    hg = GDN_HEADS_PER_STEP
    qkvz = _mm(h, w_in, (li,), n_out=cdim + vdim, tn=512, tk=h.shape[1], name="gdn_in_proj")
    ba = _mm(h, w_in[li, :, cdim + vdim:], name="gdn_gate_proj")
    beta, g = _gdn_gates(ba[:, :GDN_V_HEADS], ba[:, GDN_V_HEADS:], a_log[li], dt_bias[li])

    qkv_p = _gdn_conv_prompt(qkvz, conv_w[li], mp, seq)
    o_p, ssm_p = _gdn_rule(qkv_p, qkvz, cdim, _head_groups(g[:mp], hg), _head_groups(beta[:mp], hg),
                           norm_g[li], None, nbp, seq, GDN_CHUNK if seq % GDN_CHUNK == 0 else seq)
    conv_p = qkvz[:mp, :cdim].reshape(nbp, seq, cdim)[:, seq - (GDN_CONV - 1):]

    qkv_s = qkvz[mp:, :cdim].reshape(nbs, steps, cdim)
    win = jnp.concatenate([state_conv[li], qkv_s], axis=1)
    conv_s = win[:, win.shape[1] - (GDN_CONV - 1):]
    win_t = jnp.pad(win, ((0, 0), (0, 8 - win.shape[1] % 8 if win.shape[1] % 8 else 0), (0, 0))).transpose(1, 0, 2)
    qkv_c = _gdn_conv_sample(win_t, conv_w[li], steps).transpose(1, 0, 2).reshape(ms, cdim)
    pad = lambda a: _pad_steps(a, nbs, steps, SAMPLE_PAD)
    o_s, ssm_s = _gdn_rule(pad(qkv_c), pad(qkvz[mp:, cdim:cdim + vdim]), 0,
                           _head_groups(pad(g[mp:]), hg), _head_groups(pad(beta[mp:]), hg), norm_g[li],
                           state_ssm[li], nbs, SAMPLE_PAD, SAMPLE_PAD)
    o_s = o_s.reshape(nbs, SAMPLE_PAD, vdim)[:, :steps].reshape(ms, vdim)
    o_all = jnp.concatenate([o_p, o_s], axis=0)
    x_new = _mm(o_all, w_out, (li,), residual=x_res, name="gdn_out_proj")
    return x_new, ssm_p, conv_p, ssm_s, conv_s


def _mla_q_kernel(cq_ref, gq_ref, wn_ref, wr_ref, gn_ref, gr_ref, cos_ref, sin_ref, qn_ref, qr_ref, cqn_scr,
                  *, hg, scale):
    j = pl.program_id(1)

    @pl.when(j == 0)
    def _():
        cqn_scr[...] = _rms(cq_ref[...], gq_ref[...]).astype(BF16)

    a = cqn_scr[...]
    qn = _dot(a, wn_ref[...])
    qr = _dot(a, wr_ref[...])
    width = hg * MLA_ROPE
    half = MLA_ROPE // 2
    xg = qr * gr_ref[...]
    lane = lax.broadcasted_iota(jnp.int32, xg.shape, 1)
    swapped = jnp.where(lane % MLA_ROPE < half, pltpu.roll(xg, width - half, 1), pltpu.roll(xg, half, 1))
    roped = xg * cos_ref[...] + swapped * sin_ref[...]
    gn = gn_ref[...]
    for h in range(hg):
        ns = slice(h * MLA_NOPE, (h + 1) * MLA_NOPE)
        rs = slice(h * MLA_ROPE, (h + 1) * MLA_ROPE)
        an = qn[:, ns]
        ar = qr[:, rs]
        ss = jnp.sum(an * an, axis=-1, keepdims=True) + jnp.sum(ar * ar, axis=-1, keepdims=True)
        r = lax.rsqrt(ss / MLA_QK + EPS) * scale
        qn_ref[:, ns] = (an * r * gn).astype(qn_ref.dtype)
        qr_ref[:, rs] = (roped[:, rs] * r).astype(qr_ref.dtype)


def _mla_q(proj, q_norm_g, w_uq_nope, w_uq_rope, g_q, cos_t, sin_t):
    m = proj.shape[0]
    hg = MLA_Q_HEADS_PER_STEP
    tm = _tile(m, ROW_TILE, SUBLANE_BF16)
    gr_t = jnp.tile(g_q[MLA_NOPE:], hg).reshape(1, hg * MLA_ROPE)
    return pl.pallas_call(
        functools.partial(_mla_q_kernel, hg=hg, scale=MLA_QK ** -0.5),
        grid=(m // tm, MLA_HEADS // hg),
        in_specs=[pl.BlockSpec((tm, MLA_Q_RANK), lambda i, j: (i, 0)),
                  pl.BlockSpec((1, MLA_Q_RANK), lambda i, j: (0, 0)),
                  pl.BlockSpec((MLA_Q_RANK, hg * MLA_NOPE), lambda i, j: (0, j)),
                  pl.BlockSpec((MLA_Q_RANK, hg * MLA_ROPE), lambda i, j: (0, j)),
                  pl.BlockSpec((1, MLA_NOPE), lambda i, j: (0, 0)),
                  pl.BlockSpec((1, hg * MLA_ROPE), lambda i, j: (0, 0)),
                  pl.BlockSpec((tm, hg * MLA_ROPE), lambda i, j: (i, 0)),
                  pl.BlockSpec((tm, hg * MLA_ROPE), lambda i, j: (i, 0))],
        out_specs=[pl.BlockSpec((tm, hg * MLA_NOPE), lambda i, j: (i, j)),
                   pl.BlockSpec((tm, hg * MLA_ROPE), lambda i, j: (i, j))],
        out_shape=[jax.ShapeDtypeStruct((m, MLA_HEADS * MLA_NOPE), BF16),
                   jax.ShapeDtypeStruct((m, MLA_HEADS * MLA_ROPE), BF16)],
        scratch_shapes=[pltpu.VMEM((tm, MLA_Q_RANK), BF16)],
        compiler_params=_cparams(("parallel", "arbitrary")),
        name="mla_q",
    )(proj, q_norm_g.reshape(1, -1), w_uq_nope, w_uq_rope, g_q[:MLA_NOPE].reshape(1, -1), gr_t,
      jnp.tile(cos_t, (1, hg)), jnp.tile(sin_t, (1, hg)))


def _mla_kv_kernel(ckv_ref, kr_ref, gkv_ref, wuk_ref, wuv_ref, gkn_ref, gkr_ref, cos_ref, sin_ref, swap_ref,
                   lat_ref, kpe_ref, rinv_ref, kn_ref, krr_ref, v_ref, lat_scr, kpe_scr, *, hg):
    j = pl.program_id(1)

    @pl.when(j == 0)
    def _():
        lat = _rms(ckv_ref[...], gkv_ref[...])
        lat_ref[...] = lat
        lat_scr[...] = lat.astype(BF16)
        xg = kr_ref[...] * gkr_ref[...]
        x1, x2, x3 = _split3(xg)
        sw = swap_ref[...]
        swapped = _dot(x1, sw) + _dot(x2, sw) + _dot(x3, sw)
        kpe = xg * cos_ref[...] + swapped * sin_ref[...]
        kpe_ref[...] = kpe
        kpe_scr[...] = kpe
        rinv_ref[...] = jnp.zeros_like(rinv_ref)

    a = lat_scr[...]
    kn = _dot(a, wuk_ref[...])
    v_ref[...] = _dot(a, wuv_ref[...]).astype(v_ref.dtype)
    kr = kr_ref[...]
    ssr = jnp.sum(kr * kr, axis=-1, keepdims=True)
    kpe = kpe_scr[...]
    lane = lax.broadcasted_iota(jnp.int32, rinv_ref.shape, 1)
    racc = rinv_ref[...]
    gkn = gkn_ref[...]
    for h in range(hg):
        ns = slice(h * MLA_NOPE, (h + 1) * MLA_NOPE)
        x = kn[:, ns]
        ri = lax.rsqrt((jnp.sum(x * x, axis=-1, keepdims=True) + ssr) / MLA_QK + EPS)
        kn_ref[:, ns] = (x * gkn * ri).astype(kn_ref.dtype)
        krr_ref[:, h * MLA_ROPE:(h + 1) * MLA_ROPE] = (kpe * ri).astype(krr_ref.dtype)
        racc = jnp.where(lane == j * hg + h, ri, racc)
    rinv_ref[...] = racc


def _mla_kv(proj, k_r, kv_norm_g, w_uk2, w_uv2, g_k, cos_t, sin_t):
    m = proj.shape[0]
    hg = MLA_KV_HEADS_PER_STEP
    tm = _tile(m, ROW_TILE, SUBLANE_BF16)
    half = MLA_ROPE // 2
    idx = jnp.arange(MLA_ROPE)
    swap = (idx[:, None] == (idx[None, :] + half) % MLA_ROPE).astype(BF16)
    row = lambda w: pl.BlockSpec((tm, w), lambda i, j: (i, 0))
    vec = lambda w: pl.BlockSpec((1, w), lambda i, j: (0, 0))
    return pl.pallas_call(
        functools.partial(_mla_kv_kernel, hg=hg),
        grid=(m // tm, MLA_HEADS // hg),
        in_specs=[pl.BlockSpec((tm, MLA_KV_RANK), lambda i, j: (i, MLA_Q_RANK // MLA_KV_RANK)),
                  row(MLA_ROPE), vec(MLA_KV_RANK),
                  pl.BlockSpec((MLA_KV_RANK, hg * MLA_NOPE), lambda i, j: (0, j)),
                  pl.BlockSpec((MLA_KV_RANK, hg * MLA_V), lambda i, j: (0, j)),
                  vec(MLA_NOPE), vec(MLA_ROPE), row(MLA_ROPE), row(MLA_ROPE),
                  pl.BlockSpec((MLA_ROPE, MLA_ROPE), lambda i, j: (0, 0))],
        out_specs=[row(MLA_KV_RANK), row(MLA_ROPE), row(MLA_HEADS),
                   pl.BlockSpec((tm, hg * MLA_NOPE), lambda i, j: (i, j)),
                   pl.BlockSpec((tm, hg * MLA_ROPE), lambda i, j: (i, j)),
                   pl.BlockSpec((tm, hg * MLA_V), lambda i, j: (i, j))],
        out_shape=[jax.ShapeDtypeStruct((m, MLA_KV_RANK), F32),
                   jax.ShapeDtypeStruct((m, MLA_ROPE), F32),
                   jax.ShapeDtypeStruct((m, MLA_HEADS), F32),
                   jax.ShapeDtypeStruct((m, MLA_HEADS * MLA_NOPE), BF16),
                   jax.ShapeDtypeStruct((m, MLA_HEADS * MLA_ROPE), BF16),
                   jax.ShapeDtypeStruct((m, MLA_HEADS * MLA_V), BF16)],
        scratch_shapes=[pltpu.VMEM((tm, MLA_KV_RANK), BF16), pltpu.VMEM((tm, MLA_ROPE), F32)],
        compiler_params=_cparams(("parallel", "arbitrary")),
        name="mla_kv",
    )(proj, k_r, kv_norm_g.reshape(1, -1), w_uk2, w_uv2, g_k[:MLA_NOPE].reshape(1, -1),
      g_k[MLA_NOPE:].reshape(1, -1), cos_t, sin_t, swap)


def _flash_kernel(qn_ref, qr_ref, kn_ref, kr_ref, v_ref, o_ref, *, tq, hp):
    i = pl.program_id(2)
    row = lax.broadcasted_iota(jnp.int32, (tq, tq), 0)
    col = lax.broadcasted_iota(jnp.int32, (tq, tq), 1)
    for h in range(hp):
        ns = slice(h * MLA_NOPE, (h + 1) * MLA_NOPE)
        rs = slice(h * MLA_ROPE, (h + 1) * MLA_ROPE)
        vs = slice(h * MLA_V, (h + 1) * MLA_V)
        qn = qn_ref[:, ns]
        qr = qr_ref[:, rs]

        def body(jb, carry):
            m, l, acc = carry
            ks = pl.ds(pl.multiple_of(jb * tq, tq), tq)
            s = _dot_nt(qn, kn_ref[ks, ns]) + _dot_nt(qr, kr_ref[ks, rs])
            s = jnp.where(jnp.logical_or(jb < i, row >= col), s, -jnp.inf)
            m_new = jnp.maximum(m, jnp.max(s, axis=-1, keepdims=True))
            corr = jnp.exp(m - m_new)
            p = jnp.exp(s - m_new)
            l = l * corr + jnp.sum(p, axis=-1, keepdims=True)
            acc = acc * corr + _dot(p, v_ref[ks, vs])
            return m_new, l, acc

        init = (jnp.full((tq, 1), -jnp.inf, F32), jnp.zeros((tq, 1), F32), jnp.zeros((tq, MLA_V), F32))
        _, l, acc = lax.fori_loop(0, i + 1, body, init)
        o_ref[:, vs] = (acc / l).astype(o_ref.dtype)


def _mla_prompt_attention(qn, qr, kn, kr, v, nb, seq):
    hp = FLASH_HEADS_PER_STEP
    tq = _tile(seq, FLASH_BLOCK, SUBLANE_BF16)
    nq = seq // tq
    qmap = lambda b, h, i: (b * nq + i, h)
    kmap = lambda b, h, i: (b, h)
    return pl.pallas_call(
        functools.partial(_flash_kernel, tq=tq, hp=hp),
        grid=(nb, MLA_HEADS // hp, nq),
        in_specs=[pl.BlockSpec((tq, hp * MLA_NOPE), qmap),
                  pl.BlockSpec((tq, hp * MLA_ROPE), qmap),
                  pl.BlockSpec((seq, hp * MLA_NOPE), kmap),
                  pl.BlockSpec((seq, hp * MLA_ROPE), kmap),
                  pl.BlockSpec((seq, hp * MLA_V), kmap)],
        out_specs=pl.BlockSpec((tq, hp * MLA_V), qmap),
        out_shape=jax.ShapeDtypeStruct((nb * seq, MLA_HEADS * MLA_V), BF16),
        compiler_params=_cparams(("parallel", "parallel", "arbitrary")),
        name="mla_prompt_attention",
    )(qn, qr, kn, kr, v)


def _absorb_q_kernel(q_ref, g_ref, w_ref, o_ref):
    o_ref[...] = _dot_nt(q_ref[...].astype(F32) * g_ref[...], w_ref[...]).astype(o_ref.dtype)


def _absorb_q(qn, g_k_nope, w_uk2, mp, ms):
    assert mp % ms == 0
    return pl.pallas_call(
        _absorb_q_kernel,
        grid=(MLA_HEADS,),
        in_specs=[pl.BlockSpec((ms, MLA_NOPE), lambda h: (mp // ms, h)),
                  pl.BlockSpec((1, MLA_NOPE), lambda h: (0, 0)),
                  pl.BlockSpec((MLA_KV_RANK, MLA_NOPE), lambda h: (0, h))],
        out_specs=pl.BlockSpec((ms, MLA_KV_RANK), lambda h: (0, h)),
        out_shape=jax.ShapeDtypeStruct((ms, MLA_HEADS * MLA_KV_RANK), BF16),
        compiler_params=_cparams(("parallel",)),
        name="mla_absorb_q",
    )(qn, g_k_nope.reshape(1, -1), w_uk2)


def _expand_o_kernel(o_ref, w_ref, out_ref):
    out_ref[...] = _dot(o_ref[...], w_ref[...]).astype(out_ref.dtype)


def _expand_o(o_lat, w_uv2):
    ms = o_lat.shape[0]
    return pl.pallas_call(
        _expand_o_kernel,
        grid=(MLA_HEADS,),
        in_specs=[pl.BlockSpec((ms, MLA_KV_RANK), lambda h: (0, h)),
                  pl.BlockSpec((MLA_KV_RANK, MLA_V), lambda h: (0, h))],
        out_specs=pl.BlockSpec((ms, MLA_V), lambda h: (0, h)),
        out_shape=jax.ShapeDtypeStruct((ms, MLA_HEADS * MLA_V), BF16),
        compiler_params=_cparams(("parallel",)),
        name="mla_expand_o",
    )(o_lat, w_uv2)


def _transpose_exact(x, eye):
    x1, x2, x3 = _split3(x)
    return _dot_nt(eye, x1) + _dot_nt(eye, x2) + _dot_nt(eye, x3)


def _softmax_update(sc, values, m_scr, l_scr, acc_scr):
    m_old = m_scr[...]
    m_new = jnp.maximum(m_old, jnp.max(sc, axis=-1, keepdims=True))
    corr = jnp.exp(m_old - m_new)
    p = jnp.exp(sc - m_new)
    l_scr[...] = l_scr[...] * corr + jnp.sum(p, axis=-1, keepdims=True)
    acc_scr[...] = acc_scr[...] * corr + _dot(p, values)
    m_scr[...] = m_new


def _sample_attn_kernel(pt_ref, ql_ref, qp_ref, *refs, pps, nsteps, steps, page):
    lat_refs = refs[:pps]
    pe_refs = refs[pps:2 * pps]
    ri_refs = refs[2 * pps:3 * pps]
    latn_ref, pen_ref, rin_ref, o_ref, m_scr, l_scr, acc_scr, lat_buf, pe_buf, ri_buf = refs[3 * pps:]
    s = pl.program_id(1)

    @pl.when(s == 0)
    def _():
        m_scr[...] = jnp.full_like(m_scr, -jnp.inf)
        l_scr[...] = jnp.zeros_like(l_scr)
        acc_scr[...] = jnp.zeros_like(acc_scr)

    for p in range(pps):
        rows = slice(p * page, (p + 1) * page)
        lat_buf[rows, :] = lat_refs[p][...].astype(BF16)
        pe_buf[rows, :] = pe_refs[p][...].astype(BF16)
        ri_buf[rows, :] = ri_refs[p][...]
    ql = ql_ref[...]
    qp = qp_ref[...]
    ri = lax.broadcasted_iota(jnp.int32, (MLA_HEADS, MLA_HEADS), 0)
    ci = lax.broadcasted_iota(jnp.int32, (MLA_HEADS, MLA_HEADS), 1)
    eye = (ri == ci).astype(BF16)
    rinv_t = _transpose_exact(ri_buf[...], eye)
    lat = lat_buf[...]
    sc = (_dot_nt(ql, lat) + _dot_nt(qp, pe_buf[...])) * jnp.concatenate([rinv_t] * steps, axis=0)
    _softmax_update(sc, lat, m_scr, l_scr, acc_scr)

    @pl.when(s == nsteps - 1)
    def _():
        latn = latn_ref[...].astype(BF16)
        nk = latn.shape[0]
        rn_t = _transpose_exact(rin_ref[...], eye)
        scn = (_dot_nt(ql, latn) + _dot_nt(qp, pen_ref[...])) * jnp.concatenate([rn_t] * steps, axis=0)
        qstep = lax.broadcasted_iota(jnp.int32, scn.shape, 0) // MLA_HEADS
        kstep = lax.broadcasted_iota(jnp.int32, scn.shape, 1)
        scn = jnp.where(jnp.logical_and(kstep < steps, kstep <= qstep), scn, -jnp.inf)
        _softmax_update(scn, latn, m_scr, l_scr, acc_scr)
        o_ref[...] = (acc_scr[...] / l_scr[...]).astype(o_ref.dtype)


def _mla_sample_attention(q_lat, q_pe, lat_new, pe_new, rinv_new, cache_lat, cache_pe, cache_rinv, li,
                          page_table, steps):
    nb, n_pages = page_table.shape
    page = cache_lat.shape[2]
    pps = _tile(n_pages, PAGES_PER_STEP, 1)
    nsteps = n_pages // pps
    rows = q_lat.shape[1]
    npad = lat_new.shape[1]

    def page_spec(width, p):
        return pl.BlockSpec((None, None, page, width),
                            lambda b, s, pt: (li, pt[b * n_pages + s * pps + p], 0, 0))

    per_b = lambda r, w: pl.BlockSpec((None, r, w), lambda b, s, pt: (b, 0, 0))
    in_specs = ([per_b(rows, MLA_KV_RANK), per_b(rows, MLA_ROPE)]
                + [page_spec(MLA_KV_RANK, p) for p in range(pps)]
                + [page_spec(MLA_ROPE, p) for p in range(pps)]
                + [page_spec(MLA_HEADS, p) for p in range(pps)]
                + [per_b(npad, MLA_KV_RANK), per_b(npad, MLA_ROPE), per_b(npad, MLA_HEADS)])
    grid_spec = pltpu.PrefetchScalarGridSpec(
        num_scalar_prefetch=1,
        grid=(nb, nsteps),
        in_specs=in_specs,
        out_specs=per_b(rows, MLA_KV_RANK),
        scratch_shapes=[pltpu.VMEM((rows, 1), F32), pltpu.VMEM((rows, 1), F32),
                        pltpu.VMEM((rows, MLA_KV_RANK), F32),
                        pltpu.VMEM((pps * page, MLA_KV_RANK), BF16),
                        pltpu.VMEM((pps * page, MLA_ROPE), BF16),
                        pltpu.VMEM((pps * page, MLA_HEADS), F32)])
    return pl.pallas_call(
        functools.partial(_sample_attn_kernel, pps=pps, nsteps=nsteps, steps=steps, page=page),
        grid_spec=grid_spec,
        out_shape=jax.ShapeDtypeStruct((nb, rows, MLA_KV_RANK), BF16),
        compiler_params=_cparams(("parallel", "arbitrary")),
        name="mla_sample_attention",
    )(page_table.reshape(-1), q_lat, q_pe, *([cache_lat] * pps), *([cache_pe] * pps), *([cache_rinv] * pps),
      lat_new, pe_new.astype(BF16), rinv_new)


def _rope_tables(pos):
    half = MLA_ROPE // 2
    inv = ROPE_THETA ** (-jnp.arange(half, dtype=F32) / half)
    ang = pos.astype(F32)[:, None] * inv[None, :]
    cos, sin = jnp.cos(ang), jnp.sin(ang)
    return jnp.concatenate([cos, cos], axis=-1), jnp.concatenate([-sin, sin], axis=-1)


def _mla_mixer(h, x_res, cache_lat, cache_pe, cache_rinv, page_table, w_in, q_norm_g, kv_norm_g, w_uq, w_uk,
               w_uv, g_q, g_k, w_o, li, dims, past_len):
    mp, ms, nbp, seq, nbs, steps = dims
    pos = jnp.concatenate([jnp.tile(jnp.arange(seq), nbp), jnp.tile(past_len + jnp.arange(steps), nbs)])
    cos_t, sin_t = _rope_tables(pos)
    proj = _mm(h, w_in, (li,), name="mla_in_proj")
    w_uq3 = w_uq[li].reshape(MLA_Q_RANK, MLA_HEADS, MLA_QK)
    qn, qr = _mla_q(proj, q_norm_g[li], w_uq3[:, :, :MLA_NOPE].reshape(MLA_Q_RANK, -1),
                    w_uq3[:, :, MLA_NOPE:].reshape(MLA_Q_RANK, -1), g_q[li], cos_t, sin_t)
    w_uk2 = w_uk[li].reshape(MLA_KV_RANK, -1)
    w_uv2 = w_uv[li].reshape(MLA_KV_RANK, -1)
    lat, kpe, rinv, kn, krr, v = _mla_kv(proj, proj[:, MLA_Q_RANK + MLA_KV_RANK:], kv_norm_g[li], w_uk2, w_uv2,
                                         g_k[li], cos_t, sin_t)
    o_p = _mla_prompt_attention(qn, qr, kn, krr, v, nbp, seq)

    q_lat = _absorb_q(qn, g_k[li, :MLA_NOPE], w_uk2, mp, ms)
    padn = lambda a: jnp.pad(a[mp:].reshape(nbs, steps, -1), ((0, 0), (0, 8 - steps), (0, 0)))
    o_lat = _mla_sample_attention(q_lat.reshape(nbs, steps * MLA_HEADS, MLA_KV_RANK),
                                  qr[mp:].reshape(nbs, steps * MLA_HEADS, MLA_ROPE),
                                  padn(lat), padn(kpe), padn(rinv), cache_lat, cache_pe, cache_rinv, li,
                                  page_table, steps)
    o_s = _expand_o(o_lat.reshape(ms, MLA_HEADS * MLA_KV_RANK), w_uv2)
    x_new = _mm(jnp.concatenate([o_p, o_s], axis=0), w_o, (li,), residual=x_res, name="mla_out_proj")
    return x_new, lat, kpe, rinv


def kernel(x_prompt, x_sample, state_gdn_ssm, state_gdn_conv, cache_mla_latent, cache_mla_k_pe, cache_mla_k_rinv,
           page_table, norm_g, ffn_w_gate, ffn_w_up, ffn_w_down, gdn_w_in, gdn_conv_w, gdn_a_log, gdn_dt_bias,
           gdn_norm_g, gdn_w_out, mla_w_in, mla_q_norm_g, mla_kv_norm_g, mla_w_uq, mla_w_uk, mla_w_uv,
           mla_qk_norm_q, mla_qk_norm_k, mla_w_o):
    nbp, seq, d = x_prompt.shape
    nbs, steps, _ = x_sample.shape
    mp, ms = nbp * seq, nbs * steps
    dims = (mp, ms, nbp, seq, nbs, steps)
    depth = norm_g.shape[0]
    past_len = page_table.shape[1] * cache_mla_latent.shape[2]
    x = jnp.concatenate([x_prompt.reshape(mp, d), x_sample.reshape(ms, d)], axis=0)
    h = _norm(x, norm_g[0, 0], BF16)
    outs = {k: [] for k in ("ssm_p", "conv_p", "ssm_s", "conv_s", "lat", "pe", "rinv")}
    for layer in range(depth):
        li = layer // 2
        x = _macaron_half(x, h, ffn_w_gate, ffn_w_up, ffn_w_down, layer, 0)
        h = _norm(x, norm_g[layer, 1], BF16)
        if layer % 2 == 0:
            x, ssm_p, conv_p, ssm_s, conv_s = _gdn_mixer(
                h, x, state_gdn_ssm, state_gdn_conv, gdn_w_in, gdn_conv_w, gdn_a_log, gdn_dt_bias, gdn_norm_g,
                gdn_w_out, li, dims)
            for key, val in (("ssm_p", ssm_p), ("conv_p", conv_p), ("ssm_s", ssm_s), ("conv_s", conv_s)):
                outs[key].append(val)
        else:
            x, lat, kpe, rinv = _mla_mixer(
                h, x, cache_mla_latent, cache_mla_k_pe, cache_mla_k_rinv, page_table, mla_w_in, mla_q_norm_g,
                mla_kv_norm_g, mla_w_uq, mla_w_uk, mla_w_uv, mla_qk_norm_q, mla_qk_norm_k, mla_w_o, li, dims,
                past_len)
            for key, val in (("lat", lat), ("pe", kpe), ("rinv", rinv)):
                outs[key].append(val)
        h = _norm(x, norm_g[layer, 2], BF16)
        x = _macaron_half(x, h, ffn_w_gate, ffn_w_up, ffn_w_down, layer, 1)
        if layer + 1 < depth:
            x, h = _norm2(x, norm_g[layer, 3], norm_g[layer + 1, 0])
        else:
            x = _norm(x, norm_g[layer, 3], F32)

    def split(a):
        return a[:mp].reshape(nbp, seq, -1), a[mp:].reshape(nbs, steps, -1)

    lat_p, lat_s = zip(*[split(a) for a in outs["lat"]])
    pe_p, pe_s = zip(*[split(a) for a in outs["pe"]])
    rinv_p, rinv_s = zip(*[split(a) for a in outs["rinv"]])
    y_p, y_s = split(x)
    return (y_p, y_s,
            jnp.stack(outs["ssm_p"]), jnp.stack(outs["conv_p"]), jnp.stack(lat_p), jnp.stack(pe_p),
            jnp.stack(rinv_p),
            jnp.stack(outs["ssm_s"]), jnp.stack(outs["conv_s"]), jnp.stack(lat_s), jnp.stack(pe_s),
            jnp.stack(rinv_s))
```

```python
import functools
import math

import jax
import jax.numpy as jnp
from jax import lax
from jax.experimental import pallas as pl
from jax.experimental.pallas import tpu as pltpu

F32 = jnp.float32
BF16 = jnp.bfloat16
EPS = 1e-6
ROPE_THETA = 10000.0

GDN_K_HEADS = 16
GDN_V_HEADS = 32
GDN_HEAD = 128
GDN_CONV = 4
GDN_CHUNK = 64
MLA_HEADS = 64
MLA_Q_RANK = 1024
MLA_KV_RANK = 512
MLA_NOPE = 128
MLA_ROPE = 64
MLA_V = 128
MLA_QK = MLA_NOPE + MLA_ROPE
MLA_QK_PAD = 256

LANE = 128
SUBLANE_BF16 = 16
VMEM_LIMIT = 56 * 1024 * 1024

FFN_DOWN_K_TILE = 2816
ROW_TILE = 1088
GDN_BLOCK_ROWS = 256
GDN_GROUPS_PER_STEP = 4
MLA_Q_HEADS_PER_STEP = 8
MLA_KV_HEADS_PER_STEP = 8
FLASH_HEADS_PER_STEP = 4
FLASH_BLOCK = 256
PAGES_PER_STEP = 16
SAMPLE_SUBBLOCKS = 4
SAMPLE_PAD = 16


def _tile(n, target, align):
    if n <= target:
        return n
    t = (target // align) * align
    while t >= align:
        if n % t == 0:
            return t
        t -= align
    return n


def _cparams(sem):
    return pltpu.CompilerParams(dimension_semantics=sem, vmem_limit_bytes=VMEM_LIMIT)


def _dot(a, b):
    return jnp.dot(a.astype(BF16), b.astype(BF16), preferred_element_type=F32)


def _dot_nt(a, b):
    return lax.dot_general(a.astype(BF16), b.astype(BF16), (((1,), (1,)), ((), ())),
                           preferred_element_type=F32)


def _dot_tn(a, b):
    return lax.dot_general(a.astype(BF16), b.astype(BF16), (((0,), (0,)), ((), ())),
                           preferred_element_type=F32)


def _split3(x):
    x1 = x.astype(BF16)
    r1 = x - x1.astype(F32)
    x2 = r1.astype(BF16)
    x3 = (r1 - x2.astype(F32)).astype(BF16)
    return x1, x2, x3


def _rms(x, g):
    return x * lax.rsqrt(jnp.mean(x * x, axis=-1, keepdims=True) + EPS) * g


def _sigmoid(x):
    return 1.0 / (1.0 + jnp.exp(-x))


def _norm_kernel(x_ref, g_ref, o_ref):
    o_ref[...] = _rms(x_ref[...], g_ref[...]).astype(o_ref.dtype)


def _norm2_kernel(x_ref, g1_ref, g2_ref, y_ref, h_ref):
    y = _rms(x_ref[...], g1_ref[...])
    y_ref[...] = y
    h_ref[...] = _rms(y, g2_ref[...]).astype(h_ref.dtype)


def _norm(x, g, out_dtype, row0=0, nrows=None):
    d = x.shape[1]
    nrows = x.shape[0] if nrows is None else nrows
    tm = _tile(math.gcd(row0, nrows), 512, SUBLANE_BF16)
    return pl.pallas_call(
        _norm_kernel,
        grid=(nrows // tm,),
        in_specs=[pl.BlockSpec((tm, d), lambda i: (i + row0 // tm, 0)), pl.BlockSpec((1, d), lambda i: (0, 0))],
        out_specs=pl.BlockSpec((tm, d), lambda i: (i, 0)),
        out_shape=jax.ShapeDtypeStruct((nrows, d), out_dtype),
        compiler_params=_cparams(("parallel",)),
        name="rmsnorm",
    )(x, g.reshape(1, d))


def _norm2(x, g1, g2):
    m, d = x.shape
    tm = _tile(m, 512, SUBLANE_BF16)
    return pl.pallas_call(
        _norm2_kernel,
        grid=(m // tm,),
        in_specs=[pl.BlockSpec((tm, d), lambda i: (i, 0)), pl.BlockSpec((1, d), lambda i: (0, 0)),
                  pl.BlockSpec((1, d), lambda i: (0, 0))],
        out_specs=[pl.BlockSpec((tm, d), lambda i: (i, 0)), pl.BlockSpec((tm, d), lambda i: (i, 0))],
        out_shape=[jax.ShapeDtypeStruct((m, d), F32), jax.ShapeDtypeStruct((m, d), BF16)],
        compiler_params=_cparams(("parallel",)),
        name="rmsnorm_pair",
    )(x, g1.reshape(1, d), g2.reshape(1, d))


def _mm_kernel(*refs, has_res, nk):
    if has_res:
        a_ref, w_ref, r_ref, o_ref = refs
    else:
        a_ref, w_ref, o_ref = refs
    part = _dot(a_ref[...], w_ref[...])
    if nk == 1:
        if has_res:
            part = part + r_ref[...]
        o_ref[...] = part.astype(o_ref.dtype)
        return
    k = pl.program_id(2)

    @pl.when(k == 0)
    def _():
        o_ref[...] = part + r_ref[...] if has_res else part

    @pl.when(k > 0)
    def _():
        o_ref[...] += part


def _mm(a, w, wprefix=(), *, n_out=None, out_dtype=F32, residual=None, tm=ROW_TILE, tn=1024, tk=1024,
        name="matmul"):
    m, kdim = a.shape
    assert w.shape[-2] == kdim
    n = n_out or w.shape[-1]
    tm = _tile(m, tm, SUBLANE_BF16)
    tn = _tile(n, tn, LANE)
    tk = _tile(kdim, tk, LANE)
    nk = kdim // tk
    assert nk == 1 or out_dtype == F32
    npre = len(wprefix)
    in_specs = [pl.BlockSpec((tm, tk), lambda i, j, k: (i, k)),
                pl.BlockSpec((None,) * npre + (tk, tn), lambda i, j, k: tuple(wprefix) + (k, j))]
    args = [a, w]
    if residual is not None:
        in_specs.append(pl.BlockSpec((tm, tn), lambda i, j, k: (i, j)))
        args.append(residual)
    return pl.pallas_call(
        functools.partial(_mm_kernel, has_res=residual is not None, nk=nk),
        grid=(m // tm, n // tn, nk),
        in_specs=in_specs,
        out_specs=pl.BlockSpec((tm, tn), lambda i, j, k: (i, j)),
        out_shape=jax.ShapeDtypeStruct((m, n), out_dtype),
        compiler_params=_cparams(("parallel", "parallel", "arbitrary")),
        name=name,
    )(*args)


def _ffn_up_kernel(h_ref, wg_ref, wu_ref, o_ref, *, n_valid):
    f = pl.program_id(1)

    @pl.when(f < n_valid)
    def _():
        h = h_ref[...]
        gate = _dot(h, wg_ref[...])
        up = _dot(h, wu_ref[...])
        o_ref[...] = (0.5 * gate * _sigmoid(gate) * up).astype(o_ref.dtype)

    @pl.when(f >= n_valid)
    def _():
        o_ref[...] = jnp.zeros_like(o_ref)


def _ffn_up(h, w_gate, w_up, layer, half, f_pad):
    m, d = h.shape
    f_dim = w_gate.shape[-1]
    tm = _tile(m, ROW_TILE, SUBLANE_BF16)
    tf = _tile(f_dim, 256, LANE)
    assert f_pad % tf == 0
    n_valid = f_dim // tf

    def wmap(i, f):
        return (layer, half, 0, jnp.minimum(f, n_valid - 1))

    return pl.pallas_call(
        functools.partial(_ffn_up_kernel, n_valid=n_valid),
        grid=(m // tm, f_pad // tf),
        in_specs=[pl.BlockSpec((tm, d), lambda i, f: (i, 0)),
                  pl.BlockSpec((None, None, d, tf), wmap),
                  pl.BlockSpec((None, None, d, tf), wmap)],
        out_specs=pl.BlockSpec((tm, tf), lambda i, f: (i, f)),
        out_shape=jax.ShapeDtypeStruct((m, f_pad), BF16),
        compiler_params=_cparams(("parallel", "arbitrary")),
        name="ffn_up",
    )(h, w_gate, w_up)


def _pad_ffn_down(w_down):
    f_dim = w_down.shape[-2]
    f_pad = -(-f_dim // 1024) * 1024 if f_dim > 1024 else f_dim
    return jnp.pad(w_down.astype(BF16), ((0, 0), (0, 0), (0, f_pad - f_dim), (0, 0)))


def _macaron_half(x, h, w_gate, w_up, w_down_pad, layer, half):
    act = _ffn_up(h, w_gate, w_up, layer, half, w_down_pad.shape[-2])
    return _mm(act, w_down_pad, (layer, half), residual=x, tk=FFN_DOWN_K_TILE, name="ffn_down")


def _gates_kernel(b_ref, a_ref, alog_ref, dt_ref, beta_ref, g_ref):
    beta_ref[...] = _sigmoid(b_ref[...])
    x = a_ref[...] + dt_ref[...]
    softplus = jnp.maximum(x, 0.0) + jnp.log1p(jnp.exp(-jnp.abs(x)))
    g_ref[...] = -jnp.exp(alog_ref[...]) * softplus


def _gdn_gates(beta_in, a_in, a_log, dt_bias):
    m, h = beta_in.shape
    tm = _tile(m, 2176, 8)
    row = pl.BlockSpec((tm, h), lambda i: (i, 0))
    vec = pl.BlockSpec((1, h), lambda i: (0, 0))
    return pl.pallas_call(
        _gates_kernel,
        grid=(m // tm,),
        in_specs=[row, row, vec, vec],
        out_specs=[row, row],
        out_shape=[jax.ShapeDtypeStruct((m, h), F32)] * 2,
        compiler_params=_cparams(("parallel",)),
        name="gdn_gates",
    )(beta_in, a_in, a_log.reshape(1, h), dt_bias.reshape(1, h))


def _conv_finish(y, o_ref, row_sl, is_qk, qk_scale):
    y = y * _sigmoid(y)
    tc = y.shape[-1]

    @pl.when(is_qk)
    def _():
        for h in range(tc // GDN_HEAD):
            sl = slice(h * GDN_HEAD, (h + 1) * GDN_HEAD)
            yh = y[:, sl]
            r = lax.rsqrt(jnp.sum(yh * yh, axis=-1, keepdims=True) + EPS)
            o_ref[row_sl + (sl,)] = (yh * (r * qk_scale)).astype(o_ref.dtype)

    @pl.when(jnp.logical_not(is_qk))
    def _():
        o_ref[row_sl + (slice(None),)] = y.astype(o_ref.dtype)


def _conv_prompt_kernel(x_ref, halo_ref, w_ref, o_ref, *, tiles_per_seq, nq_tiles, nqk_tiles):
    i = pl.program_id(0)
    c = pl.program_id(1)
    x = x_ref[...]
    tt = x.shape[0]
    halo = jnp.where(i % tiles_per_seq == 0, 0.0, halo_ref[...])
    xp = jnp.concatenate([halo, x], axis=0)
    w = w_ref[...]
    hl = halo.shape[0]
    y = xp[hl - 3:hl - 3 + tt] * w[0:1]
    for j in range(1, GDN_CONV):
        y = y + xp[hl - 3 + j:hl - 3 + j + tt] * w[j:j + 1]
    qk_scale = jnp.where(c < nq_tiles, GDN_HEAD ** -0.5, 1.0)
    _conv_finish(y, o_ref, (slice(None),), c < nqk_tiles, qk_scale)


def _conv_sample_kernel(x_ref, w_ref, o_ref, *, steps, nq_tiles, nqk_tiles):
    c = pl.program_id(0)
    w = w_ref[...]
    qk_scale = jnp.where(c < nq_tiles, GDN_HEAD ** -0.5, 1.0)
    for s in range(steps):
        y = x_ref[s] * w[0:1]
        for j in range(1, GDN_CONV):
            y = y + x_ref[s + j] * w[j:j + 1]
        _conv_finish(y, o_ref, (s, slice(None)), c < nqk_tiles, qk_scale)


def _gdn_conv_prompt(qkvz, conv_w, mp, seq):
    cdim = conv_w.shape[-1]
    tt = _tile(seq, 256, 8)
    tc = _tile(2 * GDN_K_HEADS * GDN_HEAD // 2, 512, GDN_HEAD)
    nq = GDN_K_HEADS * GDN_HEAD // tc
    return pl.pallas_call(
        functools.partial(_conv_prompt_kernel, tiles_per_seq=seq // tt, nq_tiles=nq, nqk_tiles=2 * nq),
        grid=(mp // tt, cdim // tc),
        in_specs=[pl.BlockSpec((tt, tc), lambda i, c: (i, c)),
                  pl.BlockSpec((8, tc), lambda i, c: (jnp.maximum(i * (tt // 8) - 1, 0), c)),
                  pl.BlockSpec((GDN_CONV, tc), lambda i, c: (0, c))],
        out_specs=pl.BlockSpec((tt, tc), lambda i, c: (i, c)),
        out_shape=jax.ShapeDtypeStruct((mp, cdim), BF16),
        compiler_params=_cparams(("parallel", "parallel")),
        name="gdn_conv_prompt",
    )(qkvz, qkvz, conv_w)


def _gdn_conv_sample(xpad_t, conv_w, steps):
    tp, nb, cdim = xpad_t.shape
    tc = _tile(GDN_K_HEADS * GDN_HEAD, 512, GDN_HEAD)
    nq = GDN_K_HEADS * GDN_HEAD // tc
    return pl.pallas_call(
        functools.partial(_conv_sample_kernel, steps=steps, nq_tiles=nq, nqk_tiles=2 * nq),
        grid=(cdim // tc,),
        in_specs=[pl.BlockSpec((tp, nb, tc), lambda c: (0, 0, c)),
                  pl.BlockSpec((GDN_CONV, tc), lambda c: (0, c))],
        out_specs=pl.BlockSpec((steps, nb, tc), lambda c: (0, 0, c)),
        out_shape=jax.ShapeDtypeStruct((steps, nb, cdim), BF16),
        compiler_params=_cparams(("parallel",)),
        name="gdn_conv_sample",
    )(xpad_t, conv_w)


def _gdn_chunk_kernel(*refs, c, hb, ngroups, rep, nchunks, has_s0):
    if has_s0:
        q_ref, k_ref, v_ref, z_ref, g_ref, b_ref, ng_ref, s0_ref, o_ref, sfin_ref, s_scr = refs
    else:
        q_ref, k_ref, v_ref, z_ref, g_ref, b_ref, ng_ref, o_ref, sfin_ref, s_scr = refs
    n = pl.program_id(2)

    @pl.when(n == 0)
    def _():
        s_scr[...] = s0_ref[...] if has_s0 else jnp.zeros_like(s_scr)

    d = GDN_HEAD
    r = hb * c
    rowi = lax.broadcasted_iota(jnp.int32, (r, r), 0)
    coli = lax.broadcasted_iota(jnp.int32, (r, r), 1)
    same = (rowi ^ coli) < c
    tril = jnp.logical_and(same, rowi >= coli)
    strict = jnp.logical_and(same, rowi > coli)
    triu = jnp.logical_and(same, rowi <= coli)
    eye = rowi == coli
    eye_f = eye.astype(F32)
    head_of_row = lax.broadcasted_iota(jnp.int32, (r, hb), 0) // c == lax.broadcasted_iota(jnp.int32, (r, hb), 1)
    ng = ng_ref[...]

    def stack(ref, heads):
        return jnp.concatenate([ref[:, h * d:(h + 1) * d] for h in heads], axis=0)

    def stacked_column(ref, j0):
        tiled = jnp.concatenate([ref[:, j0:j0 + hb]] * hb, axis=0)
        return jnp.sum(jnp.where(head_of_row, tiled, 0.0), axis=1, keepdims=True)

    groups = range(ngroups)
    vheads = [list(range(gi * hb, (gi + 1) * hb)) for gi in groups]
    qst = [stack(q_ref, [h // rep for h in vheads[gi]]) for gi in groups]
    kst = [stack(k_ref, [h // rep for h in vheads[gi]]) for gi in groups]
    kf = [kst[gi].astype(F32) for gi in groups]
    vf = [stack(v_ref, vheads[gi]).astype(F32) for gi in groups]
    g = [stacked_column(g_ref, gi * hb) for gi in groups]
    beta = [stacked_column(b_ref, gi * hb) for gi in groups]
    qk_kk = [_dot_nt(jnp.concatenate([qst[gi], kst[gi]], axis=0), kst[gi]) for gi in groups]
    g_row = [jnp.sum(jnp.where(eye, g[gi], 0.0), axis=0, keepdims=True) for gi in groups]
    gc_col = [jnp.sum(jnp.where(tril, g_row[gi], 0.0), axis=1, keepdims=True) for gi in groups]
    gc_row = [jnp.sum(jnp.where(triu, g[gi], 0.0), axis=0, keepdims=True) for gi in groups]
    decay = [jnp.where(tril, jnp.exp(jnp.where(tril, gc_col[gi] - gc_row[gi], 0.0)), 0.0) for gi in groups]
    qk = [qk_kk[gi][:r] * decay[gi] for gi in groups]
    mm = [-jnp.where(strict, qk_kk[gi][r:] * beta[gi] * decay[gi], 0.0) for gi in groups]
    inv = [eye_f + mm[gi] for gi in groups]
    for _ in range(int(math.log2(c)) - 1):
        mm = [_dot(mm[gi], mm[gi]) for gi in groups]
        inv = [inv[gi] + _dot(inv[gi], mm[gi]) for gi in groups]
    e_col = [jnp.exp(gc_col[gi]) for gi in groups]
    uw = [_dot(inv[gi], jnp.concatenate([vf[gi] * beta[gi], kf[gi] * (beta[gi] * e_col[gi])], axis=1))
          for gi in groups]
    qd = [qst[gi].astype(F32) * e_col[gi] for gi in groups]
    heads = [(gi, j) for gi in groups for j in range(hb)]
    rows = [slice(j * c, (j + 1) * c) for j in range(hb)]
    s_old = {(gi, j): s_scr[gi * hb + j] for gi, j in heads}
    wq = {(gi, j): _dot(jnp.concatenate([uw[gi][rows[j], d:], qd[gi][rows[j]]], axis=0), s_old[gi, j])
          for gi, j in heads}
    v_new = {(gi, j): uw[gi][rows[j], :d] - wq[gi, j][:c] for gi, j in heads}
    for gi, j in heads:
        gc_last = gc_col[gi][(j + 1) * c - 1:(j + 1) * c, :]
        k_dec = kf[gi][rows[j]] * jnp.exp(gc_last - gc_col[gi][rows[j]])
        s_scr[gi * hb + j] = s_old[gi, j] * jnp.exp(gc_last) + _dot_tn(k_dec, v_new[gi, j])
    o = [jnp.concatenate([wq[gi, j][c:] for j in range(hb)], axis=0)
         + _dot(qk[gi], jnp.concatenate([v_new[gi, j] for j in range(hb)], axis=0)) for gi in groups]
    for gi in groups:
        z = stack(z_ref, vheads[gi])
        out = (_rms(o[gi], ng) * (z * _sigmoid(z))).astype(o_ref.dtype)
        for j in range(hb):
            o_ref[:, (gi * hb + j) * d:(gi * hb + j + 1) * d] = out[rows[j]]

    @pl.when(n == nchunks - 1)
    def _():
        sfin_ref[...] = s_scr[...]


def _gdn_rule(qkv, z, z_col0, g, beta, norm_g, s0, nb, seq, c):
    d = GDN_HEAD
    hb = min(GDN_BLOCK_ROWS // c, GDN_V_HEADS)
    ngroups = min(GDN_GROUPS_PER_STEP, GDN_V_HEADS // hb)
    hs = hb * ngroups
    rep = GDN_V_HEADS // GDN_K_HEADS
    nchunks = seq // c
    koff = GDN_K_HEADS // (hs // rep)
    voff = 2 * GDN_K_HEADS // hs
    zoff = z_col0 // (hs * d)
    rowmap = lambda b, h, n: (b * nchunks + n, h)
    gatemap = lambda b, h, n: (h, b * nchunks + n, 0)
    in_specs = [pl.BlockSpec((c, hs // rep * d), rowmap),
                pl.BlockSpec((c, hs // rep * d), lambda b, h, n: (b * nchunks + n, koff + h)),
                pl.BlockSpec((c, hs * d), lambda b, h, n: (b * nchunks + n, voff + h)),
                pl.BlockSpec((c, hs * d), lambda b, h, n: (b * nchunks + n, zoff + h)),
                pl.BlockSpec((None, c, hs), gatemap),
                pl.BlockSpec((None, c, hs), gatemap),
                pl.BlockSpec((1, d), lambda b, h, n: (0, 0))]
    args = [qkv, qkv, qkv, z, _head_groups(g, hs), _head_groups(beta, hs), norm_g.reshape(1, d)]
    if s0 is not None:
        in_specs.append(pl.BlockSpec((None, hs, d, d), lambda b, h, n: (b, h, 0, 0)))
        args.append(s0)
    return pl.pallas_call(
        functools.partial(_gdn_chunk_kernel, c=c, hb=hb, ngroups=ngroups, rep=rep, nchunks=nchunks,
                          has_s0=s0 is not None),
        grid=(nb, GDN_V_HEADS // hs, nchunks),
        in_specs=in_specs,
        out_specs=[pl.BlockSpec((c, hs * d), rowmap),
                   pl.BlockSpec((None, hs, d, d), lambda b, h, n: (b, h, 0, 0))],
        out_shape=[jax.ShapeDtypeStruct((nb * seq, GDN_V_HEADS * d), BF16),
                   jax.ShapeDtypeStruct((nb, GDN_V_HEADS, d, d), F32)],
        scratch_shapes=[pltpu.VMEM((hs, d, d), F32)],
        compiler_params=_cparams(("parallel", "parallel", "arbitrary")),
        name="gdn_rule",
    )(*args)


def _head_groups(x, hg):
    r, h = x.shape
    return x.reshape(r, h // hg, hg).transpose(1, 0, 2)


def _pad_steps(x, nb, steps, pad_to):
    cdim = x.shape[-1]
    x = x.reshape(nb, steps, cdim)
    return jnp.pad(x, ((0, 0), (0, pad_to - steps), (0, 0))).reshape(nb * pad_to, cdim)


def _gdn_mixer(h, x_res, state_ssm, state_conv, w_in, conv_w, a_log, dt_bias, norm_g, w_out, li, dims):
    mp, ms, nbp, seq, nbs, steps = dims
    cdim = conv_w.shape[-1]
    vdim = GDN_V_HEADS * GDN_HEAD
    qkvz = _mm(h, w_in, (li,), n_out=cdim + vdim, tn=512, tk=h.shape[1], name="gdn_in_proj")
    ba = _mm(h, w_in[li, :, cdim + vdim:], name="gdn_gate_proj")
    beta, g = _gdn_gates(ba[:, :GDN_V_HEADS], ba[:, GDN_V_HEADS:], a_log[li], dt_bias[li])

    qkv_p = _gdn_conv_prompt(qkvz, conv_w[li], mp, seq)
    o_p, ssm_p = _gdn_rule(qkv_p, qkvz, cdim, g[:mp], beta[:mp], norm_g[li], None, nbp, seq,
                           GDN_CHUNK if seq % GDN_CHUNK == 0 else seq)
    conv_p = jnp.stack([qkvz[(b + 1) * seq - (GDN_CONV - 1):(b + 1) * seq, :cdim] for b in range(nbp)])

    qkv_s = qkvz[mp:, :cdim].reshape(nbs, steps, cdim)
    win = jnp.concatenate([state_conv[li], qkv_s], axis=1)
    conv_s = win[:, win.shape[1] - (GDN_CONV - 1):]
    win_t = jnp.pad(win, ((0, 0), (0, 8 - win.shape[1] % 8 if win.shape[1] % 8 else 0), (0, 0))).transpose(1, 0, 2)
    qkv_c = _gdn_conv_sample(win_t, conv_w[li], steps).transpose(1, 0, 2).reshape(ms, cdim)
    pad = lambda a: _pad_steps(a, nbs, steps, SAMPLE_PAD)
    o_s, ssm_s = _gdn_rule(pad(qkv_c), pad(qkvz[mp:, cdim:cdim + vdim]), 0, pad(g[mp:]), pad(beta[mp:]),
                           norm_g[li], state_ssm[li], nbs, SAMPLE_PAD, SAMPLE_PAD)
    o_s = o_s.reshape(nbs, SAMPLE_PAD, vdim)[:, :steps].reshape(ms, vdim)
    o_all = jnp.concatenate([o_p, o_s], axis=0)
    x_new = _mm(o_all, w_out, (li,), residual=x_res, tk=2048, name="gdn_out_proj")
    return x_new, ssm_p, conv_p, ssm_s, conv_s


def _mla_q_kernel(cq_ref, gq_ref, wn_ref, wr_ref, gn_ref, gr_ref, cos_ref, sin_ref, qc_ref, qr_ref, cqn_scr,
                  *, hg, scale):
    j = pl.program_id(1)

    @pl.when(j == 0)
    def _():
        cqn_scr[...] = _rms(cq_ref[...], gq_ref[...]).astype(BF16)

    a = cqn_scr[...]
    qn = _dot(a, wn_ref[...])
    qr = _dot(a, wr_ref[...])
    width = hg * MLA_ROPE
    half = MLA_ROPE // 2
    xg = qr * gr_ref[...]
    lane = lax.broadcasted_iota(jnp.int32, xg.shape, 1)
    swapped = jnp.where(lane % MLA_ROPE < half, pltpu.roll(xg, width - half, 1), pltpu.roll(xg, half, 1))
    roped = xg * cos_ref[...] + swapped * sin_ref[...]
    gn = gn_ref[...]
    for h in range(hg):
        ns = slice(h * MLA_NOPE, (h + 1) * MLA_NOPE)
        rs = slice(h * MLA_ROPE, (h + 1) * MLA_ROPE)
        an = qn[:, ns]
        ar = qr[:, rs]
        ss = jnp.sum(an * an, axis=-1, keepdims=True) + jnp.sum(ar * ar, axis=-1, keepdims=True)
        r = lax.rsqrt(ss / MLA_QK + EPS) * scale
        q_rope = (roped[:, rs] * r).astype(qr_ref.dtype)
        base = h * MLA_QK_PAD
        qc_ref[:, base:base + MLA_NOPE] = (an * r * gn).astype(qc_ref.dtype)
        qc_ref[:, base + MLA_NOPE:base + MLA_QK] = q_rope
        qc_ref[:, base + MLA_QK:base + MLA_QK_PAD] = jnp.zeros((an.shape[0], MLA_QK_PAD - MLA_QK), qc_ref.dtype)
        qr_ref[:, rs] = q_rope


def _mla_q(proj, q_norm_g, w_uq_nope, w_uq_rope, g_q, cos_t, sin_t):
    m = proj.shape[0]
    hg = MLA_Q_HEADS_PER_STEP
    tm = _tile(m, ROW_TILE, SUBLANE_BF16)
    gr_t = jnp.tile(g_q[MLA_NOPE:], hg).reshape(1, hg * MLA_ROPE)
    return pl.pallas_call(
        functools.partial(_mla_q_kernel, hg=hg, scale=MLA_QK ** -0.5),
        grid=(m // tm, MLA_HEADS // hg),
        in_specs=[pl.BlockSpec((tm, MLA_Q_RANK), lambda i, j: (i, 0)),
                  pl.BlockSpec((1, MLA_Q_RANK), lambda i, j: (0, 0)),
                  pl.BlockSpec((MLA_Q_RANK, hg * MLA_NOPE), lambda i, j: (0, j)),
                  pl.BlockSpec((MLA_Q_RANK, hg * MLA_ROPE), lambda i, j: (0, j)),
                  pl.BlockSpec((1, MLA_NOPE), lambda i, j: (0, 0)),
                  pl.BlockSpec((1, hg * MLA_ROPE), lambda i, j: (0, 0)),
                  pl.BlockSpec((tm, hg * MLA_ROPE), lambda i, j: (i, 0)),
                  pl.BlockSpec((tm, hg * MLA_ROPE), lambda i, j: (i, 0))],
        out_specs=[pl.BlockSpec((tm, hg * MLA_QK_PAD), lambda i, j: (i, j)),
                   pl.BlockSpec((tm, hg * MLA_ROPE), lambda i, j: (i, j))],
        out_shape=[jax.ShapeDtypeStruct((m, MLA_HEADS * MLA_QK_PAD), BF16),
                   jax.ShapeDtypeStruct((m, MLA_HEADS * MLA_ROPE), BF16)],
        scratch_shapes=[pltpu.VMEM((tm, MLA_Q_RANK), BF16)],
        compiler_params=_cparams(("parallel", "arbitrary")),
        name="mla_q",
    )(proj, q_norm_g.reshape(1, -1), w_uq_nope, w_uq_rope, g_q[:MLA_NOPE].reshape(1, -1), gr_t,
      jnp.tile(cos_t, (1, hg)), jnp.tile(sin_t, (1, hg)))


def _mla_kv_kernel(ckv_ref, kr_ref, gkv_ref, wuk_ref, wuv_ref, gkn_ref, gkr_ref, cos_ref, sin_ref, swap_ref,
                   lat_ref, kpe_ref, rinv_ref, kc_ref, v_ref, lat_scr, kpe_scr, *, hg):
    j = pl.program_id(1)

    @pl.when(j == 0)
    def _():
        lat = _rms(ckv_ref[...], gkv_ref[...])
        lat_ref[...] = lat
        lat_scr[...] = lat.astype(BF16)
        xg = kr_ref[...] * gkr_ref[...]
        x1, x2, x3 = _split3(xg)
        sw = swap_ref[...]
        swapped = _dot(x1, sw) + _dot(x2, sw) + _dot(x3, sw)
        kpe = xg * cos_ref[...] + swapped * sin_ref[...]
        kpe_ref[...] = kpe
        kpe_scr[...] = kpe
        rinv_ref[...] = jnp.zeros_like(rinv_ref)

    a = lat_scr[...]
    kn = _dot(a, wuk_ref[...])
    v_ref[...] = _dot(a, wuv_ref[...]).astype(v_ref.dtype)
    kr = kr_ref[...]
    ssr = jnp.sum(kr * kr, axis=-1, keepdims=True)
    kpe = kpe_scr[...]
    lane = lax.broadcasted_iota(jnp.int32, rinv_ref.shape, 1)
    racc = rinv_ref[...]
    gkn = gkn_ref[...]
    for h in range(hg):
        ns = slice(h * MLA_NOPE, (h + 1) * MLA_NOPE)
        x = kn[:, ns]
        ri = lax.rsqrt((jnp.sum(x * x, axis=-1, keepdims=True) + ssr) / MLA_QK + EPS)
        base = h * MLA_QK_PAD
        kc_ref[:, base:base + MLA_NOPE] = (x * gkn * ri).astype(kc_ref.dtype)
        kc_ref[:, base + MLA_NOPE:base + MLA_QK] = (kpe * ri).astype(kc_ref.dtype)
        kc_ref[:, base + MLA_QK:base + MLA_QK_PAD] = jnp.zeros((x.shape[0], MLA_QK_PAD - MLA_QK), kc_ref.dtype)
        racc = jnp.where(lane == j * hg + h, ri, racc)
    rinv_ref[...] = racc


def _mla_kv(proj, k_r, kv_norm_g, w_uk2, w_uv2, g_k, cos_t, sin_t):
    m = proj.shape[0]
    hg = MLA_KV_HEADS_PER_STEP
    tm = _tile(m, ROW_TILE, SUBLANE_BF16)
    half = MLA_ROPE // 2
    idx = jnp.arange(MLA_ROPE)
    swap = (idx[:, None] == (idx[None, :] + half) % MLA_ROPE).astype(BF16)
    row = lambda w: pl.BlockSpec((tm, w), lambda i, j: (i, 0))
    vec = lambda w: pl.BlockSpec((1, w), lambda i, j: (0, 0))
    return pl.pallas_call(
        functools.partial(_mla_kv_kernel, hg=hg),
        grid=(m // tm, MLA_HEADS // hg),
        in_specs=[pl.BlockSpec((tm, MLA_KV_RANK), lambda i, j: (i, MLA_Q_RANK // MLA_KV_RANK)),
                  row(MLA_ROPE), vec(MLA_KV_RANK),
                  pl.BlockSpec((MLA_KV_RANK, hg * MLA_NOPE), lambda i, j: (0, j)),
                  pl.BlockSpec((MLA_KV_RANK, hg * MLA_V), lambda i, j: (0, j)),
                  vec(MLA_NOPE), vec(MLA_ROPE), row(MLA_ROPE), row(MLA_ROPE),
                  pl.BlockSpec((MLA_ROPE, MLA_ROPE), lambda i, j: (0, 0))],
        out_specs=[row(MLA_KV_RANK), row(MLA_ROPE), row(MLA_HEADS),
                   pl.BlockSpec((tm, hg * MLA_QK_PAD), lambda i, j: (i, j)),
                   pl.BlockSpec((tm, hg * MLA_V), lambda i, j: (i, j))],
        out_shape=[jax.ShapeDtypeStruct((m, MLA_KV_RANK), F32),
                   jax.ShapeDtypeStruct((m, MLA_ROPE), F32),
                   jax.ShapeDtypeStruct((m, MLA_HEADS), F32),
                   jax.ShapeDtypeStruct((m, MLA_HEADS * MLA_QK_PAD), BF16),
                   jax.ShapeDtypeStruct((m, MLA_HEADS * MLA_V), BF16)],
        scratch_shapes=[pltpu.VMEM((tm, MLA_KV_RANK), BF16), pltpu.VMEM((tm, MLA_ROPE), F32)],
        compiler_params=_cparams(("parallel", "arbitrary")),
        name="mla_kv",
    )(proj, k_r, kv_norm_g.reshape(1, -1), w_uk2, w_uv2, g_k[:MLA_NOPE].reshape(1, -1),
      g_k[MLA_NOPE:].reshape(1, -1), cos_t, sin_t, swap)


def _flash_kernel(q_ref, k_ref, v_ref, o_ref, *, tq, hp):
    i = pl.program_id(2)
    krow = lax.broadcasted_iota(jnp.int32, (tq, tq), 0)
    qcol = lax.broadcasted_iota(jnp.int32, (tq, tq), 1)
    heads = range(hp)
    csl = [slice(h * MLA_QK_PAD, (h + 1) * MLA_QK_PAD) for h in heads]
    vsl = [slice(h * MLA_V, (h + 1) * MLA_V) for h in heads]
    q = [q_ref[:, csl[h]] for h in heads]

    def block(jb, carry, on_diagonal):
        ks = pl.ds(pl.multiple_of(jb * tq, tq), tq)
        s = [_dot_nt(k_ref[ks, csl[h]], q[h]) for h in heads]
        if on_diagonal:
            s = [jnp.where(krow <= qcol, s[h], -jnp.inf) for h in heads]
        m_new = [jnp.maximum(carry[h][0], jnp.max(s[h], axis=0, keepdims=True)) for h in heads]
        corr = [jnp.exp(carry[h][0] - m_new[h]) for h in heads]
        p = [jnp.exp(s[h] - m_new[h]) for h in heads]
        l = [carry[h][1] * corr[h] + jnp.sum(p[h], axis=0, keepdims=True) for h in heads]
        acc = [carry[h][2] * corr[h] + _dot_tn(v_ref[ks, vsl[h]], p[h]) for h in heads]
        return tuple((m_new[h], l[h], acc[h]) for h in heads)

    init = tuple((jnp.full((1, tq), -jnp.inf, F32), jnp.zeros((1, tq), F32), jnp.zeros((MLA_V, tq), F32))
                 for _ in heads)
    carry = lax.fori_loop(0, i, lambda jb, c: block(jb, c, False), init)
    carry = block(i, carry, True)
    ri = lax.broadcasted_iota(jnp.int32, (MLA_V, MLA_V), 0)
    ci = lax.broadcasted_iota(jnp.int32, (MLA_V, MLA_V), 1)
    eye = (ri == ci).astype(BF16)
    for h in heads:
        _, l, acc = carry[h]
        o_ref[:, vsl[h]] = _dot_tn((acc / l).astype(BF16), eye).astype(o_ref.dtype)


def _mla_prompt_attention(qc, kc, v, nb, seq):
    hp = FLASH_HEADS_PER_STEP
    tq = _tile(seq, FLASH_BLOCK, SUBLANE_BF16)
    nq = seq // tq
    qmap = lambda b, h, i: (b * nq + i, h)
    kmap = lambda b, h, i: (b, h)
    return pl.pallas_call(
        functools.partial(_flash_kernel, tq=tq, hp=hp),
        grid=(nb, MLA_HEADS // hp, nq),
        in_specs=[pl.BlockSpec((tq, hp * MLA_QK_PAD), qmap),
                  pl.BlockSpec((seq, hp * MLA_QK_PAD), kmap),
                  pl.BlockSpec((seq, hp * MLA_V), kmap)],
        out_specs=pl.BlockSpec((tq, hp * MLA_V), qmap),
        out_shape=jax.ShapeDtypeStruct((nb * seq, MLA_HEADS * MLA_V), BF16),
        compiler_params=_cparams(("parallel", "parallel", "arbitrary")),
        name="mla_prompt_attention",
    )(qc, kc, v)


def _absorb_q_kernel(q_ref, g_ref, w_ref, o_ref):
    o_ref[...] = _dot_nt(q_ref[...].astype(F32) * g_ref[...], w_ref[...]).astype(o_ref.dtype)


def _absorb_q(qc, g_k_nope, w_uk2, mp, ms):
    assert mp % ms == 0
    return pl.pallas_call(
        _absorb_q_kernel,
        grid=(MLA_HEADS,),
        in_specs=[pl.BlockSpec((ms, MLA_NOPE), lambda h: (mp // ms, h * (MLA_QK_PAD // MLA_NOPE))),
                  pl.BlockSpec((1, MLA_NOPE), lambda h: (0, 0)),
                  pl.BlockSpec((MLA_KV_RANK, MLA_NOPE), lambda h: (0, h))],
        out_specs=pl.BlockSpec((ms, MLA_KV_RANK), lambda h: (0, h)),
        out_shape=jax.ShapeDtypeStruct((ms, MLA_HEADS * MLA_KV_RANK), BF16),
        compiler_params=_cparams(("parallel",)),
        name="mla_absorb_q",
    )(qc, g_k_nope.reshape(1, -1), w_uk2)


def _expand_o_kernel(o_ref, w_ref, out_ref):
    out_ref[...] = _dot(o_ref[...], w_ref[...]).astype(out_ref.dtype)


def _expand_o(o_lat, w_uv2):
    ms = o_lat.shape[0]
    return pl.pallas_call(
        _expand_o_kernel,
        grid=(MLA_HEADS,),
        in_specs=[pl.BlockSpec((ms, MLA_KV_RANK), lambda h: (0, h)),
                  pl.BlockSpec((MLA_KV_RANK, MLA_V), lambda h: (0, h))],
        out_specs=pl.BlockSpec((ms, MLA_V), lambda h: (0, h)),
        out_shape=jax.ShapeDtypeStruct((ms, MLA_HEADS * MLA_V), BF16),
        compiler_params=_cparams(("parallel",)),
        name="mla_expand_o",
    )(o_lat, w_uv2)


def _softmax_update(sc, values, m_scr, l_scr, acc_scr):
    m_old = m_scr[...]
    m_new = jnp.maximum(m_old, jnp.max(sc, axis=-1, keepdims=True))
    corr = jnp.exp(m_old - m_new)
    p = jnp.exp(sc - m_new)
    l_scr[...] = l_scr[...] * corr + jnp.sum(p, axis=-1, keepdims=True)
    acc_scr[...] = acc_scr[...] * corr + _dot(p, values)
    m_scr[...] = m_new


def _sample_attn_kernel(pt_ref, ql_ref, qp_ref, *refs, pps, nsteps, steps, page):
    lat_refs = refs[:pps]
    pe_refs = refs[pps:2 * pps]
    ri_refs = refs[2 * pps:3 * pps]
    latn_ref, pen_ref, rin_ref, o_ref, m_scr, l_scr, acc_scr, lat_buf, pe_buf, ri_buf = refs[3 * pps:]
    s = pl.program_id(1)

    @pl.when(s == 0)
    def _():
        m_scr[...] = jnp.full_like(m_scr, -jnp.inf)
        l_scr[...] = jnp.zeros_like(l_scr)
        acc_scr[...] = jnp.zeros_like(acc_scr)

    for p in range(pps):
        keys = slice(p * page, (p + 1) * page)
        lat_buf[keys, :] = lat_refs[p][...].astype(BF16)
        pe_buf[:, keys] = pe_refs[p][...].astype(BF16)
        ri_buf[:, keys] = ri_refs[p][...]
    ql = ql_ref[...]
    qp = qp_ref[...]
    nsub = SAMPLE_SUBBLOCKS if pps % SAMPLE_SUBBLOCKS == 0 else 1
    width = pps * page // nsub
    subs = [slice(u * width, (u + 1) * width) for u in range(nsub)]
    lat = [lat_buf[sl, :] for sl in subs]
    sc = [(_dot_nt(ql, lat[u]) + _dot(qp, pe_buf[:, subs[u]])) * jnp.concatenate([ri_buf[:, subs[u]]] * steps, axis=0)
          for u in range(nsub)]
    for u in range(nsub):
        _softmax_update(sc[u], lat[u], m_scr, l_scr, acc_scr)

    @pl.when(s == nsteps - 1)
    def _():
        latn = latn_ref[...].astype(BF16)
        scn = (_dot_nt(ql, latn) + _dot(qp, pen_ref[...])) * jnp.concatenate([rin_ref[...]] * steps, axis=0)
        qstep = lax.broadcasted_iota(jnp.int32, scn.shape, 0) // MLA_HEADS
        kstep = lax.broadcasted_iota(jnp.int32, scn.shape, 1)
        scn = jnp.where(jnp.logical_and(kstep < steps, kstep <= qstep), scn, -jnp.inf)
        _softmax_update(scn, latn, m_scr, l_scr, acc_scr)
        o_ref[...] = (acc_scr[...] / l_scr[...]).astype(o_ref.dtype)


def _mla_sample_attention(q_lat, q_pe, lat_new, pe_new_t, rinv_new_t, cache_lat, cache_pe_t, cache_rinv_t, li,
                          page_table, steps):
    nb, n_pages = page_table.shape
    page = cache_lat.shape[2]
    pps = _tile(n_pages, PAGES_PER_STEP, 1)
    nsteps = n_pages // pps
    rows = q_lat.shape[1]
    npad = lat_new.shape[1]

    def page_spec(shape, p):
        return pl.BlockSpec((None, None) + shape, lambda b, s, pt: (li, pt[b * n_pages + s * pps + p], 0, 0))

    per_b = lambda r, w: pl.BlockSpec((None, r, w), lambda b, s, pt: (b, 0, 0))
    in_specs = ([per_b(rows, MLA_KV_RANK), per_b(rows, MLA_ROPE)]
                + [page_spec((page, MLA_KV_RANK), p) for p in range(pps)]
                + [page_spec((MLA_ROPE, page), p) for p in range(pps)]
                + [page_spec((MLA_HEADS, page), p) for p in range(pps)]
                + [per_b(npad, MLA_KV_RANK), per_b(MLA_ROPE, npad), per_b(MLA_HEADS, npad)])
    grid_spec = pltpu.PrefetchScalarGridSpec(
        num_scalar_prefetch=1,
        grid=(nb, nsteps),
        in_specs=in_specs,
        out_specs=per_b(rows, MLA_KV_RANK),
        scratch_shapes=[pltpu.VMEM((rows, 1), F32), pltpu.VMEM((rows, 1), F32),
                        pltpu.VMEM((rows, MLA_KV_RANK), F32),
                        pltpu.VMEM((pps * page, MLA_KV_RANK), BF16),
                        pltpu.VMEM((MLA_ROPE, pps * page), BF16),
                        pltpu.VMEM((MLA_HEADS, pps * page), F32)])
    return pl.pallas_call(
        functools.partial(_sample_attn_kernel, pps=pps, nsteps=nsteps, steps=steps, page=page),
        grid_spec=grid_spec,
        out_shape=jax.ShapeDtypeStruct((nb, rows, MLA_KV_RANK), BF16),
        compiler_params=_cparams(("parallel", "arbitrary")),
        name="mla_sample_attention",
    )(page_table.reshape(-1), q_lat, q_pe, *([cache_lat] * pps), *([cache_pe_t] * pps), *([cache_rinv_t] * pps),
      lat_new, pe_new_t.astype(BF16), rinv_new_t)


def _rope_tables(pos):
    half = MLA_ROPE // 2
    inv = ROPE_THETA ** (-jnp.arange(half, dtype=F32) / half)
    ang = pos.astype(F32)[:, None] * inv[None, :]
    cos, sin = jnp.cos(ang), jnp.sin(ang)
    return jnp.concatenate([cos, cos], axis=-1), jnp.concatenate([-sin, sin], axis=-1)


def _mla_mixer(h, x_res, cache_lat, cache_pe, cache_rinv, page_table, w_in, q_norm_g, kv_norm_g, w_uq, w_uk,
               w_uv, g_q, g_k, w_o, li, dims, past_len):
    mp, ms, nbp, seq, nbs, steps = dims
    pos = jnp.concatenate([jnp.tile(jnp.arange(seq), nbp), jnp.tile(past_len + jnp.arange(steps), nbs)])
    cos_t, sin_t = _rope_tables(pos)
    proj = _mm(h, w_in, (li,), name="mla_in_proj")
    w_uq3 = w_uq[li].reshape(MLA_Q_RANK, MLA_HEADS, MLA_QK)
    qc, qr = _mla_q(proj, q_norm_g[li], w_uq3[:, :, :MLA_NOPE].reshape(MLA_Q_RANK, -1),
                    w_uq3[:, :, MLA_NOPE:].reshape(MLA_Q_RANK, -1), g_q[li], cos_t, sin_t)
    w_uk2 = w_uk[li].reshape(MLA_KV_RANK, -1)
    w_uv2 = w_uv[li].reshape(MLA_KV_RANK, -1)
    lat, kpe, rinv, kc, v = _mla_kv(proj, proj[:, MLA_Q_RANK + MLA_KV_RANK:], kv_norm_g[li], w_uk2, w_uv2,
                                         g_k[li], cos_t, sin_t)
    o_p = _mla_prompt_attention(qc, kc, v, nbp, seq)

    q_lat = _absorb_q(qc, g_k[li, :MLA_NOPE], w_uk2, mp, ms)
    padn = lambda a: jnp.pad(a[mp:].reshape(nbs, steps, -1), ((0, 0), (0, 8 - steps), (0, 0)))
    o_lat = _mla_sample_attention(q_lat.reshape(nbs, steps * MLA_HEADS, MLA_KV_RANK),
                                  qr[mp:].reshape(nbs, steps * MLA_HEADS, MLA_ROPE),
                                  padn(lat), padn(kpe).swapaxes(1, 2), padn(rinv).swapaxes(1, 2), cache_lat,
                                  cache_pe.swapaxes(2, 3), cache_rinv.swapaxes(2, 3), li, page_table, steps)
    o_s = _expand_o(o_lat.reshape(ms, MLA_HEADS * MLA_KV_RANK), w_uv2)
    x_new = _mm(jnp.concatenate([o_p, o_s], axis=0), w_o, (li,), residual=x_res, tk=2048, name="mla_out_proj")
    return x_new, lat, kpe, rinv


def kernel(x_prompt, x_sample, state_gdn_ssm, state_gdn_conv, cache_mla_latent, cache_mla_k_pe, cache_mla_k_rinv,
           page_table, norm_g, ffn_w_gate, ffn_w_up, ffn_w_down, gdn_w_in, gdn_conv_w, gdn_a_log, gdn_dt_bias,
           gdn_norm_g, gdn_w_out, mla_w_in, mla_q_norm_g, mla_kv_norm_g, mla_w_uq, mla_w_uk, mla_w_uv,
           mla_qk_norm_q, mla_qk_norm_k, mla_w_o):
    nbp, seq, d = x_prompt.shape
    nbs, steps, _ = x_sample.shape
    mp, ms = nbp * seq, nbs * steps
    dims = (mp, ms, nbp, seq, nbs, steps)
    depth = norm_g.shape[0]
    past_len = page_table.shape[1] * cache_mla_latent.shape[2]
    x = jnp.concatenate([x_prompt.reshape(mp, d), x_sample.reshape(ms, d)], axis=0)
    w_down_pad = _pad_ffn_down(ffn_w_down)
    h = _norm(x, norm_g[0, 0], BF16)
    outs = {k: [] for k in ("ssm_p", "conv_p", "ssm_s", "conv_s", "lat", "pe", "rinv")}
    for layer in range(depth):
        li = layer // 2
        x = _macaron_half(x, h, ffn_w_gate, ffn_w_up, w_down_pad, layer, 0)
        h = _norm(x, norm_g[layer, 1], BF16)
        if layer % 2 == 0:
            x, ssm_p, conv_p, ssm_s, conv_s = _gdn_mixer(
                h, x, state_gdn_ssm, state_gdn_conv, gdn_w_in, gdn_conv_w, gdn_a_log, gdn_dt_bias, gdn_norm_g,
                gdn_w_out, li, dims)
            for key, val in (("ssm_p", ssm_p), ("conv_p", conv_p), ("ssm_s", ssm_s), ("conv_s", conv_s)):
                outs[key].append(val)
        else:
            x, lat, kpe, rinv = _mla_mixer(
                h, x, cache_mla_latent, cache_mla_k_pe, cache_mla_k_rinv, page_table, mla_w_in, mla_q_norm_g,
                mla_kv_norm_g, mla_w_uq, mla_w_uk, mla_w_uv, mla_qk_norm_q, mla_qk_norm_k, mla_w_o, li, dims,
                past_len)
            for key, val in (("lat", lat), ("pe", kpe), ("rinv", rinv)):
                outs[key].append(val)
        h = _norm(x, norm_g[layer, 2], BF16)
        x = _macaron_half(x, h, ffn_w_gate, ffn_w_up, w_down_pad, layer, 1)
        if layer + 1 < depth:
            x, h = _norm2(x, norm_g[layer, 3], norm_g[layer + 1, 0])
        else:
            y_p = _norm(x, norm_g[layer, 3], F32, 0, mp).reshape(nbp, seq, d)
            y_s = _norm(x, norm_g[layer, 3], F32, mp, ms).reshape(nbs, steps, d)

    def split(a):
        return a[:mp].reshape(nbp, seq, -1), a[mp:].reshape(nbs, steps, -1)

    lat_p, lat_s = zip(*[split(a) for a in outs["lat"]])
    pe_p, pe_s = zip(*[split(a) for a in outs["pe"]])
    rinv_p, rinv_s = zip(*[split(a) for a in outs["rinv"]])
    return (y_p, y_s,
            jnp.stack(outs["ssm_p"]), jnp.stack(outs["conv_p"]), jnp.stack(lat_p), jnp.stack(pe_p),
            jnp.stack(rinv_p),
            jnp.stack(outs["ssm_s"]), jnp.stack(outs["conv_s"]), jnp.stack(lat_s), jnp.stack(pe_s),
            jnp.stack(rinv_s))
```

```python
import functools
import math

import jax
import jax.numpy as jnp
from jax import lax
from jax.experimental import pallas as pl
from jax.experimental.pallas import tpu as pltpu

F32 = jnp.float32
BF16 = jnp.bfloat16
EPS = 1e-6
ROPE_THETA = 10000.0

GDN_K_HEADS = 16
GDN_V_HEADS = 32
GDN_HEAD = 128
GDN_CONV = 4
GDN_CHUNK = 64
MLA_HEADS = 64
MLA_Q_RANK = 1024
MLA_KV_RANK = 512
MLA_NOPE = 128
MLA_ROPE = 64
MLA_V = 128
MLA_QK = MLA_NOPE + MLA_ROPE
MLA_QK_PAD = 256

LANE = 128
SUBLANE_BF16 = 16
VMEM_LIMIT = 56 * 1024 * 1024

FFN_DOWN_K_TILE = 2688
FFN_UP_ROW_TILE = 2176
ROW_TILE = 1088
GDN_BLOCK_ROWS = 256
GDN_GROUPS_PER_STEP = 4
MLA_Q_HEADS_PER_STEP = 8
MLA_KV_HEADS_PER_STEP = 8
FLASH_HEADS_PER_STEP = 8
FLASH_BLOCK = 256
PAGES_PER_STEP = 16
SAMPLE_SUBBLOCKS = 4
SAMPLE_PAD = 16


def _tile(n, target, align):
    if n <= target:
        return n
    t = (target // align) * align
    while t >= align:
        if n % t == 0:
            return t
        t -= align
    return n


def _cparams(sem):
    return pltpu.CompilerParams(dimension_semantics=sem, vmem_limit_bytes=VMEM_LIMIT)


def _dot(a, b):
    return jnp.dot(a.astype(BF16), b.astype(BF16), preferred_element_type=F32)


def _dot_nt(a, b):
    return lax.dot_general(a.astype(BF16), b.astype(BF16), (((1,), (1,)), ((), ())),
                           preferred_element_type=F32)


def _dot_tn(a, b):
    return lax.dot_general(a.astype(BF16), b.astype(BF16), (((0,), (0,)), ((), ())),
                           preferred_element_type=F32)


def _split3(x):
    x1 = x.astype(BF16)
    r1 = x - x1.astype(F32)
    x2 = r1.astype(BF16)
    x3 = (r1 - x2.astype(F32)).astype(BF16)
    return x1, x2, x3


def _rms(x, g):
    return x * lax.rsqrt(jnp.mean(x * x, axis=-1, keepdims=True) + EPS) * g


def _sigmoid(x):
    return 1.0 / (1.0 + jnp.exp(-x))


def _norm_kernel(x_ref, g_ref, o_ref):
    o_ref[...] = _rms(x_ref[...], g_ref[...]).astype(o_ref.dtype)


def _norm2_kernel(x_ref, g1_ref, g2_ref, y_ref, h_ref):
    y = _rms(x_ref[...], g1_ref[...])
    y_ref[...] = y
    h_ref[...] = _rms(y, g2_ref[...]).astype(h_ref.dtype)


def _norm(x, g, out_dtype, row0=0, nrows=None):
    d = x.shape[1]
    nrows = x.shape[0] if nrows is None else nrows
    tm = _tile(math.gcd(row0, nrows), 512, SUBLANE_BF16)
    return pl.pallas_call(
        _norm_kernel,
        grid=(nrows // tm,),
        in_specs=[pl.BlockSpec((tm, d), lambda i: (i + row0 // tm, 0)), pl.BlockSpec((1, d), lambda i: (0, 0))],
        out_specs=pl.BlockSpec((tm, d), lambda i: (i, 0)),
        out_shape=jax.ShapeDtypeStruct((nrows, d), out_dtype),
        compiler_params=_cparams(("parallel",)),
        name="rmsnorm",
    )(x, g.reshape(1, d))


def _norm2(x, g1, g2):
    m, d = x.shape
    tm = _tile(m, 512, SUBLANE_BF16)
    return pl.pallas_call(
        _norm2_kernel,
        grid=(m // tm,),
        in_specs=[pl.BlockSpec((tm, d), lambda i: (i, 0)), pl.BlockSpec((1, d), lambda i: (0, 0)),
                  pl.BlockSpec((1, d), lambda i: (0, 0))],
        out_specs=[pl.BlockSpec((tm, d), lambda i: (i, 0)), pl.BlockSpec((tm, d), lambda i: (i, 0))],
        out_shape=[jax.ShapeDtypeStruct((m, d), F32), jax.ShapeDtypeStruct((m, d), BF16)],
        compiler_params=_cparams(("parallel",)),
        name="rmsnorm_pair",
    )(x, g1.reshape(1, d), g2.reshape(1, d))


def _mm_kernel(*refs, has_res, nk, w_is_nk):
    if has_res:
        a_ref, w_ref, r_ref, o_ref = refs
    else:
        a_ref, w_ref, o_ref = refs
    part = (_dot_nt if w_is_nk else _dot)(a_ref[...], w_ref[...])
    if nk == 1:
        if has_res:
            part = part + r_ref[...]
        o_ref[...] = part.astype(o_ref.dtype)
        return
    k = pl.program_id(2)

    @pl.when(k == 0)
    def _():
        o_ref[...] = part + r_ref[...] if has_res else part

    @pl.when(k > 0)
    def _():
        o_ref[...] += part


def _mm(a, w, wprefix=(), *, n_out=None, out_dtype=F32, residual=None, tm=ROW_TILE, tn=1024, tk=1024,
        w_is_nk=False, name="matmul"):
    m, kdim = a.shape
    assert w.shape[-1 if w_is_nk else -2] == kdim
    n = n_out or w.shape[-2 if w_is_nk else -1]
    tm = _tile(m, tm, SUBLANE_BF16)
    tn = _tile(n, tn, LANE)
    tk = _tile(kdim, tk, LANE)
    nk = kdim // tk
    assert nk == 1 or out_dtype == F32
    npre = len(wprefix)
    in_specs = [pl.BlockSpec((tm, tk), lambda i, j, k: (i, k)),
                pl.BlockSpec((None,) * npre + (tn, tk), lambda i, j, k: tuple(wprefix) + (j, k)) if w_is_nk else
                pl.BlockSpec((None,) * npre + (tk, tn), lambda i, j, k: tuple(wprefix) + (k, j))]
    args = [a, w]
    if residual is not None:
        in_specs.append(pl.BlockSpec((tm, tn), lambda i, j, k: (i, j)))
        args.append(residual)
    return pl.pallas_call(
        functools.partial(_mm_kernel, has_res=residual is not None, nk=nk, w_is_nk=w_is_nk),
        grid=(m // tm, n // tn, nk),
        in_specs=in_specs,
        out_specs=pl.BlockSpec((tm, tn), lambda i, j, k: (i, j)),
        out_shape=jax.ShapeDtypeStruct((m, n), out_dtype),
        compiler_params=_cparams(("parallel", "parallel", "arbitrary")),
        name=name,
    )(*args)


def _ffn_up_kernel(h_ref, wg_ref, wu_ref, o_ref):
    h = h_ref[...]
    gate = _dot(h, wg_ref[...])
    up = _dot(h, wu_ref[...])
    o_ref[...] = (0.5 * gate * _sigmoid(gate) * up).astype(o_ref.dtype)


def _ffn_up(h, w_gate, w_up, layer, half):
    m, d = h.shape
    f_dim = w_gate.shape[-1]
    tm = _tile(m, FFN_UP_ROW_TILE, SUBLANE_BF16)
    tf = _tile(f_dim, 256, LANE)
    wspec = pl.BlockSpec((None, None, d, tf), lambda i, f: (layer, half, 0, f))
    return pl.pallas_call(
        _ffn_up_kernel,
        grid=(m // tm, f_dim // tf),
        in_specs=[pl.BlockSpec((tm, d), lambda i, f: (i, 0), pipeline_mode=pl.Buffered(1)), wspec, wspec],
        out_specs=pl.BlockSpec((tm, tf), lambda i, f: (i, f)),
        out_shape=jax.ShapeDtypeStruct((m, f_dim), BF16),
        compiler_params=_cparams(("parallel", "arbitrary")),
        name="ffn_up",
    )(h, w_gate, w_up)


def _split_contraction(kdim, target):
    tk = _tile(kdim, target, LANE)
    if kdim // tk <= 8:
        return tk, kdim // tk, 0
    for tail in (2 * LANE, LANE):
        main = kdim - tail
        tk = _tile(main, target, LANE)
        if main % tail == 0 and main // tk <= 8:
            return tk, main // tk, tail
    raise ValueError(f"no contraction tiling for {kdim}")


def _ffn_down_kernel(a_ref, w_ref, at_ref, wt_ref, r_ref, o_ref):
    k = pl.program_id(2)
    part = _dot(a_ref[...], w_ref[...])

    @pl.when(k == 0)
    def _():
        o_ref[...] = r_ref[...] + part + _dot(at_ref[...], wt_ref[...])

    @pl.when(k > 0)
    def _():
        o_ref[...] += part


def _macaron_half(x, h, w_gate, w_up, w_down_bf16, layer, half):
    act = _ffn_up(h, w_gate, w_up, layer, half)
    m, f_dim = act.shape
    n = w_down_bf16.shape[-1]
    tk, nk, tail = _split_contraction(f_dim, FFN_DOWN_K_TILE)
    if tail == 0:
        return _mm(act, w_down_bf16, (layer, half), residual=x, tk=tk, name="ffn_down")
    tm = _tile(m, ROW_TILE, SUBLANE_BF16)
    tn = _tile(n, 1024, LANE)
    tail_blk = nk * tk // tail
    return pl.pallas_call(
        _ffn_down_kernel,
        grid=(m // tm, n // tn, nk),
        in_specs=[pl.BlockSpec((tm, tk), lambda i, j, k: (i, k)),
                  pl.BlockSpec((None, None, tk, tn), lambda i, j, k: (layer, half, k, j)),
                  pl.BlockSpec((tm, tail), lambda i, j, k: (i, tail_blk)),
                  pl.BlockSpec((None, None, tail, tn), lambda i, j, k: (layer, half, tail_blk, j)),
                  pl.BlockSpec((tm, tn), lambda i, j, k: (i, j))],
        out_specs=pl.BlockSpec((tm, tn), lambda i, j, k: (i, j)),
        out_shape=jax.ShapeDtypeStruct((m, n), F32),
        compiler_params=_cparams(("parallel", "parallel", "arbitrary")),
        name="ffn_down",
    )(act, w_down_bf16, act, w_down_bf16, x)


def _gates_kernel(b_ref, a_ref, alog_ref, dt_ref, beta_ref, g_ref):
    beta_ref[...] = _sigmoid(b_ref[...])
    x = a_ref[...] + dt_ref[...]
    softplus = jnp.maximum(x, 0.0) + jnp.log1p(jnp.exp(-jnp.abs(x)))
    g_ref[...] = -jnp.exp(alog_ref[...]) * softplus


def _gdn_gates(beta_in, a_in, a_log, dt_bias):
    m, h = beta_in.shape
    tm = _tile(m, 2176, 8)
    row = pl.BlockSpec((tm, h), lambda i: (i, 0))
    vec = pl.BlockSpec((1, h), lambda i: (0, 0))
    return pl.pallas_call(
        _gates_kernel,
        grid=(m // tm,),
        in_specs=[row, row, vec, vec],
        out_specs=[row, row],
        out_shape=[jax.ShapeDtypeStruct((m, h), F32)] * 2,
        compiler_params=_cparams(("parallel",)),
        name="gdn_gates",
    )(beta_in, a_in, a_log.reshape(1, h), dt_bias.reshape(1, h))


def _conv_finish(y, o_ref, row_sl, is_qk, qk_scale):
    y = y * _sigmoid(y)
    tc = y.shape[-1]

    @pl.when(is_qk)
    def _():
        for h in range(tc // GDN_HEAD):
            sl = slice(h * GDN_HEAD, (h + 1) * GDN_HEAD)
            yh = y[:, sl]
            r = lax.rsqrt(jnp.sum(yh * yh, axis=-1, keepdims=True) + EPS)
            o_ref[row_sl + (sl,)] = (yh * (r * qk_scale)).astype(o_ref.dtype)

    @pl.when(jnp.logical_not(is_qk))
    def _():
        o_ref[row_sl + (slice(None),)] = y.astype(o_ref.dtype)


def _conv_prompt_kernel(x_ref, halo_ref, w_ref, o_ref, *, tiles_per_seq, nq_tiles, nqk_tiles):
    i = pl.program_id(0)
    c = pl.program_id(1)
    x = x_ref[...]
    tt = x.shape[0]
    halo = jnp.where(i % tiles_per_seq == 0, 0.0, halo_ref[...])
    xp = jnp.concatenate([halo, x], axis=0)
    w = w_ref[...]
    hl = halo.shape[0]
    y = pltpu.roll(xp, GDN_CONV - 1, 0)[hl:] * w[0:1]
    for j in range(1, GDN_CONV - 1):
        y = y + pltpu.roll(xp, GDN_CONV - 1 - j, 0)[hl:] * w[j:j + 1]
    y = y + x * w[GDN_CONV - 1:GDN_CONV]
    qk_scale = jnp.where(c < nq_tiles, GDN_HEAD ** -0.5, 1.0)
    _conv_finish(y, o_ref, (slice(None),), c < nqk_tiles, qk_scale)


def _conv_sample_kernel(x_ref, w_ref, o_ref, *, steps, nq_tiles, nqk_tiles):
    c = pl.program_id(0)
    w = w_ref[...]
    qk_scale = jnp.where(c < nq_tiles, GDN_HEAD ** -0.5, 1.0)
    for s in range(steps):
        y = x_ref[s] * w[0:1]
        for j in range(1, GDN_CONV):
            y = y + x_ref[s + j] * w[j:j + 1]
        _conv_finish(y, o_ref, (s, slice(None)), c < nqk_tiles, qk_scale)


def _gdn_conv_prompt(qkvz, conv_w, mp, seq):
    cdim = conv_w.shape[-1]
    tt = _tile(seq, 256, 8)
    tc = _tile(2 * GDN_K_HEADS * GDN_HEAD // 2, 512, GDN_HEAD)
    nq = GDN_K_HEADS * GDN_HEAD // tc
    return pl.pallas_call(
        functools.partial(_conv_prompt_kernel, tiles_per_seq=seq // tt, nq_tiles=nq, nqk_tiles=2 * nq),
        grid=(mp // tt, cdim // tc),
        in_specs=[pl.BlockSpec((tt, tc), lambda i, c: (i, c)),
                  pl.BlockSpec((8, tc), lambda i, c: (jnp.maximum(i * (tt // 8) - 1, 0), c)),
                  pl.BlockSpec((GDN_CONV, tc), lambda i, c: (0, c))],
        out_specs=pl.BlockSpec((tt, tc), lambda i, c: (i, c)),
        out_shape=jax.ShapeDtypeStruct((mp, cdim), BF16),
        compiler_params=_cparams(("parallel", "parallel")),
        name="gdn_conv_prompt",
    )(qkvz, qkvz, conv_w)


def _gdn_conv_sample(xpad_t, conv_w, steps):
    tp, nb, cdim = xpad_t.shape
    tc = _tile(GDN_K_HEADS * GDN_HEAD, 512, GDN_HEAD)
    nq = GDN_K_HEADS * GDN_HEAD // tc
    return pl.pallas_call(
        functools.partial(_conv_sample_kernel, steps=steps, nq_tiles=nq, nqk_tiles=2 * nq),
        grid=(cdim // tc,),
        in_specs=[pl.BlockSpec((tp, nb, tc), lambda c: (0, 0, c)),
                  pl.BlockSpec((GDN_CONV, tc), lambda c: (0, c))],
        out_specs=pl.BlockSpec((steps, nb, tc), lambda c: (0, 0, c)),
        out_shape=jax.ShapeDtypeStruct((steps, nb, cdim), BF16),
        compiler_params=_cparams(("parallel",)),
        name="gdn_conv_sample",
    )(xpad_t, conv_w)


def _gdn_chunk_kernel(*refs, c, hb, ngroups, rep, nchunks, has_s0):
    if has_s0:
        q_ref, k_ref, v_ref, z_ref, g_ref, b_ref, ng_ref, s0_ref, o_ref, sfin_ref, s_scr = refs
    else:
        q_ref, k_ref, v_ref, z_ref, g_ref, b_ref, ng_ref, o_ref, sfin_ref, s_scr = refs
    n = pl.program_id(2)

    @pl.when(n == 0)
    def _():
        s_scr[...] = s0_ref[...] if has_s0 else jnp.zeros_like(s_scr)

    d = GDN_HEAD
    r = hb * c
    rowi = lax.broadcasted_iota(jnp.int32, (r, r), 0)
    coli = lax.broadcasted_iota(jnp.int32, (r, r), 1)
    same = (rowi ^ coli) < c
    tril = jnp.logical_and(same, rowi >= coli)
    strict = jnp.logical_and(same, rowi > coli)
    triu = jnp.logical_and(same, rowi <= coli)
    eye = rowi == coli
    eye_f = eye.astype(F32)
    head_of_row = lax.broadcasted_iota(jnp.int32, (r, hb), 0) // c == lax.broadcasted_iota(jnp.int32, (r, hb), 1)
    ng = ng_ref[...]

    def stack(ref, heads):
        return jnp.concatenate([ref[:, h * d:(h + 1) * d] for h in heads], axis=0)

    def stacked_column(ref, j0):
        tiled = jnp.concatenate([ref[:, j0:j0 + hb]] * hb, axis=0)
        return jnp.sum(jnp.where(head_of_row, tiled, 0.0), axis=1, keepdims=True)

    groups = range(ngroups)
    vheads = [list(range(gi * hb, (gi + 1) * hb)) for gi in groups]
    qst = [stack(q_ref, [h // rep for h in vheads[gi]]) for gi in groups]
    kst = [stack(k_ref, [h // rep for h in vheads[gi]]) for gi in groups]
    kf = [kst[gi].astype(F32) for gi in groups]
    vf = [stack(v_ref, vheads[gi]).astype(F32) for gi in groups]
    g = [stacked_column(g_ref, gi * hb) for gi in groups]
    beta = [stacked_column(b_ref, gi * hb) for gi in groups]
    qk_kk = [_dot_nt(jnp.concatenate([qst[gi], kst[gi]], axis=0), kst[gi]) for gi in groups]
    g_row = [jnp.sum(jnp.where(eye, g[gi], 0.0), axis=0, keepdims=True) for gi in groups]
    gc_col = [jnp.sum(jnp.where(tril, g_row[gi], 0.0), axis=1, keepdims=True) for gi in groups]
    gc_row = [jnp.sum(jnp.where(triu, g[gi], 0.0), axis=0, keepdims=True) for gi in groups]
    decay = [jnp.where(tril, jnp.exp(jnp.where(tril, gc_col[gi] - gc_row[gi], 0.0)), 0.0) for gi in groups]
    qk = [qk_kk[gi][:r] * decay[gi] for gi in groups]
    mm = [-jnp.where(strict, qk_kk[gi][r:] * beta[gi] * decay[gi], 0.0) for gi in groups]
    inv = [eye_f + mm[gi] for gi in groups]
    for _ in range(int(math.log2(c)) - 1):
        mm = [_dot(mm[gi], mm[gi]) for gi in groups]
        inv = [inv[gi] + _dot(inv[gi], mm[gi]) for gi in groups]
    e_col = [jnp.exp(gc_col[gi]) for gi in groups]
    uw = [_dot(inv[gi], jnp.concatenate([vf[gi] * beta[gi], kf[gi] * (beta[gi] * e_col[gi])], axis=1))
          for gi in groups]
    qd = [qst[gi].astype(F32) * e_col[gi] for gi in groups]
    heads = [(gi, j) for gi in groups for j in range(hb)]
    rows = [slice(j * c, (j + 1) * c) for j in range(hb)]
    s_old = {(gi, j): s_scr[gi * hb + j] for gi, j in heads}
    wq = {(gi, j): _dot(jnp.concatenate([uw[gi][rows[j], d:], qd[gi][rows[j]]], axis=0), s_old[gi, j])
          for gi, j in heads}
    v_new = {(gi, j): uw[gi][rows[j], :d] - wq[gi, j][:c] for gi, j in heads}
    for gi, j in heads:
        gc_last = gc_col[gi][(j + 1) * c - 1:(j + 1) * c, :]
        k_dec = kf[gi][rows[j]] * jnp.exp(gc_last - gc_col[gi][rows[j]])
        s_scr[gi * hb + j] = s_old[gi, j] * jnp.exp(gc_last) + _dot_tn(k_dec, v_new[gi, j])
    o = [jnp.concatenate([wq[gi, j][c:] for j in range(hb)], axis=0)
         + _dot(qk[gi], jnp.concatenate([v_new[gi, j] for j in range(hb)], axis=0)) for gi in groups]
    for gi in groups:
        z = stack(z_ref, vheads[gi])
        out = (_rms(o[gi], ng) * (z * _sigmoid(z))).astype(o_ref.dtype)
        for j in range(hb):
            o_ref[:, (gi * hb + j) * d:(gi * hb + j + 1) * d] = out[rows[j]]

    @pl.when(n == nchunks - 1)
    def _():
        sfin_ref[...] = s_scr[...]


def _gdn_rule(qkv, z, z_col0, g, beta, norm_g, s0, nb, seq, c):
    d = GDN_HEAD
    hb = min(GDN_BLOCK_ROWS // c, GDN_V_HEADS)
    ngroups = min(GDN_GROUPS_PER_STEP, GDN_V_HEADS // hb)
    hs = hb * ngroups
    rep = GDN_V_HEADS // GDN_K_HEADS
    nchunks = seq // c
    koff = GDN_K_HEADS // (hs // rep)
    voff = 2 * GDN_K_HEADS // hs
    zoff = z_col0 // (hs * d)
    rowmap = lambda b, h, n: (b * nchunks + n, h)
    gatemap = lambda b, h, n: (h, b * nchunks + n, 0)
    in_specs = [pl.BlockSpec((c, hs // rep * d), rowmap),
                pl.BlockSpec((c, hs // rep * d), lambda b, h, n: (b * nchunks + n, koff + h)),
                pl.BlockSpec((c, hs * d), lambda b, h, n: (b * nchunks + n, voff + h)),
                pl.BlockSpec((c, hs * d), lambda b, h, n: (b * nchunks + n, zoff + h)),
                pl.BlockSpec((None, c, hs), gatemap),
                pl.BlockSpec((None, c, hs), gatemap),
                pl.BlockSpec((1, d), lambda b, h, n: (0, 0))]
    args = [qkv, qkv, qkv, z, _head_groups(g, hs), _head_groups(beta, hs), norm_g.reshape(1, d)]
    if s0 is not None:
        in_specs.append(pl.BlockSpec((None, hs, d, d), lambda b, h, n: (b, h, 0, 0)))
        args.append(s0)
    return pl.pallas_call(
        functools.partial(_gdn_chunk_kernel, c=c, hb=hb, ngroups=ngroups, rep=rep, nchunks=nchunks,
                          has_s0=s0 is not None),
        grid=(nb, GDN_V_HEADS // hs, nchunks),
        in_specs=in_specs,
        out_specs=[pl.BlockSpec((c, hs * d), rowmap),
                   pl.BlockSpec((None, hs, d, d), lambda b, h, n: (b, h, 0, 0))],
        out_shape=[jax.ShapeDtypeStruct((nb * seq, GDN_V_HEADS * d), BF16),
                   jax.ShapeDtypeStruct((nb, GDN_V_HEADS, d, d), F32)],
        scratch_shapes=[pltpu.VMEM((hs, d, d), F32)],
        compiler_params=_cparams(("parallel", "parallel", "arbitrary")),
        name="gdn_rule",
    )(*args)


def _head_groups(x, hg):
    r, h = x.shape
    return x.reshape(r, h // hg, hg).transpose(1, 0, 2)


def _pad_steps(x, nb, steps, pad_to):
    cdim = x.shape[-1]
    x = x.reshape(nb, steps, cdim)
    return jnp.pad(x, ((0, 0), (0, pad_to - steps), (0, 0))).reshape(nb * pad_to, cdim)


def _gdn_mixer(h, x_res, state_ssm, state_conv, w_in, conv_w, a_log, dt_bias, norm_g, w_out, li, dims):
    mp, ms, nbp, seq, nbs, steps = dims
    cdim = conv_w.shape[-1]
    vdim = GDN_V_HEADS * GDN_HEAD
    w_in_t = jnp.swapaxes(w_in, 1, 2)
    qkvz = _mm(h, w_in_t, (li,), n_out=cdim + vdim, tn=512, tk=h.shape[1], w_is_nk=True, name="gdn_in_proj")
    ba = _mm(h, w_in_t[li, cdim + vdim:], w_is_nk=True, name="gdn_gate_proj")
    beta, g = _gdn_gates(ba[:, :GDN_V_HEADS], ba[:, GDN_V_HEADS:], a_log[li], dt_bias[li])

    qkv_p = _gdn_conv_prompt(qkvz, conv_w[li], mp, seq)
    o_p, ssm_p = _gdn_rule(qkv_p, qkvz, cdim, g[:mp], beta[:mp], norm_g[li], None, nbp, seq,
                           GDN_CHUNK if seq % GDN_CHUNK == 0 else seq)
    conv_p = jnp.stack([qkvz[(b + 1) * seq - (GDN_CONV - 1):(b + 1) * seq, :cdim] for b in range(nbp)])

    qkv_s = qkvz[mp:, :cdim].reshape(nbs, steps, cdim)
    win = jnp.concatenate([state_conv[li], qkv_s], axis=1)
    conv_s = win[:, win.shape[1] - (GDN_CONV - 1):]
    win_t = jnp.pad(win, ((0, 0), (0, 8 - win.shape[1] % 8 if win.shape[1] % 8 else 0), (0, 0))).transpose(1, 0, 2)
    qkv_c = _gdn_conv_sample(win_t, conv_w[li], steps).transpose(1, 0, 2).reshape(ms, cdim)
    pad = lambda a: _pad_steps(a, nbs, steps, SAMPLE_PAD)
    o_s, ssm_s = _gdn_rule(pad(qkv_c), pad(qkvz[mp:, cdim:cdim + vdim]), 0, pad(g[mp:]), pad(beta[mp:]),
                           norm_g[li], state_ssm[li], nbs, SAMPLE_PAD, SAMPLE_PAD)
    o_s = o_s.reshape(nbs, SAMPLE_PAD, vdim)[:, :steps].reshape(ms, vdim)
    o_all = jnp.concatenate([o_p, o_s], axis=0)
    x_new = _mm(o_all, w_out, (li,), residual=x_res, tk=2048, name="gdn_out_proj")
    return x_new, ssm_p, conv_p, ssm_s, conv_s


def _mla_q_kernel(cq_ref, gq_ref, wn_ref, wr_ref, gn_ref, gr_ref, cos_ref, sin_ref, en_ref, er_ref, qc_ref, qr_ref,
                  cqn_scr, *, hg, scale):
    j = pl.program_id(1)

    @pl.when(j == 0)
    def _():
        cqn_scr[...] = _rms(cq_ref[...], gq_ref[...]).astype(BF16)

    a = cqn_scr[...]
    qn = _dot(a, wn_ref[...])
    qr = _dot(a, wr_ref[...])
    width = hg * MLA_ROPE
    half = MLA_ROPE // 2
    xg = qr * gr_ref[...]
    lane = lax.broadcasted_iota(jnp.int32, xg.shape, 1)
    swapped = jnp.where(lane % MLA_ROPE < half, pltpu.roll(xg, width - half, 1), pltpu.roll(xg, half, 1))
    roped = xg * cos_ref[...] + swapped * sin_ref[...]
    gn = gn_ref[...]
    ss = _dot(qn * qn, en_ref[...]) + _dot(qr * qr, er_ref[...])
    r_all = lax.rsqrt(ss / MLA_QK + EPS) * scale
    for h in range(hg):
        ns = slice(h * MLA_NOPE, (h + 1) * MLA_NOPE)
        rs = slice(h * MLA_ROPE, (h + 1) * MLA_ROPE)
        an = qn[:, ns]
        r = r_all[:, h:h + 1]
        q_rope = (roped[:, rs] * r).astype(qr_ref.dtype)
        base = h * MLA_QK_PAD
        qc_ref[:, base:base + MLA_NOPE] = (an * r * gn).astype(qc_ref.dtype)
        qc_ref[:, base + MLA_NOPE:base + MLA_QK] = q_rope
        qc_ref[:, base + MLA_QK:base + MLA_QK_PAD] = jnp.zeros((an.shape[0], MLA_QK_PAD - MLA_QK), qc_ref.dtype)
        qr_ref[:, rs] = q_rope


def _mla_q(proj, q_norm_g, w_uq_nope, w_uq_rope, g_q, cos_t, sin_t):
    m = proj.shape[0]
    hg = MLA_Q_HEADS_PER_STEP
    tm = _tile(m, ROW_TILE, SUBLANE_BF16)
    gr_t = jnp.tile(g_q[MLA_NOPE:], hg).reshape(1, hg * MLA_ROPE)
    head_of = lambda width: (jnp.arange(hg * width)[:, None] // width == jnp.arange(hg)[None, :]).astype(BF16)
    return pl.pallas_call(
        functools.partial(_mla_q_kernel, hg=hg, scale=MLA_QK ** -0.5 * math.log2(math.e)),
        grid=(m // tm, MLA_HEADS // hg),
        in_specs=[pl.BlockSpec((tm, MLA_Q_RANK), lambda i, j: (i, 0)),
                  pl.BlockSpec((1, MLA_Q_RANK), lambda i, j: (0, 0)),
                  pl.BlockSpec((MLA_Q_RANK, hg * MLA_NOPE), lambda i, j: (0, j)),
                  pl.BlockSpec((MLA_Q_RANK, hg * MLA_ROPE), lambda i, j: (0, j)),
                  pl.BlockSpec((1, MLA_NOPE), lambda i, j: (0, 0)),
                  pl.BlockSpec((1, hg * MLA_ROPE), lambda i, j: (0, 0)),
                  pl.BlockSpec((tm, hg * MLA_ROPE), lambda i, j: (i, 0)),
                  pl.BlockSpec((tm, hg * MLA_ROPE), lambda i, j: (i, 0)),
                  pl.BlockSpec((hg * MLA_NOPE, hg), lambda i, j: (0, 0)),
                  pl.BlockSpec((hg * MLA_ROPE, hg), lambda i, j: (0, 0))],
        out_specs=[pl.BlockSpec((tm, hg * MLA_QK_PAD), lambda i, j: (i, j)),
                   pl.BlockSpec((tm, hg * MLA_ROPE), lambda i, j: (i, j))],
        out_shape=[jax.ShapeDtypeStruct((m, MLA_HEADS * MLA_QK_PAD), BF16),
                   jax.ShapeDtypeStruct((m, MLA_HEADS * MLA_ROPE), BF16)],
        scratch_shapes=[pltpu.VMEM((tm, MLA_Q_RANK), BF16)],
        compiler_params=_cparams(("parallel", "arbitrary")),
        name="mla_q",
    )(proj, q_norm_g.reshape(1, -1), w_uq_nope, w_uq_rope, g_q[:MLA_NOPE].reshape(1, -1), gr_t,
      jnp.tile(cos_t, (1, hg)), jnp.tile(sin_t, (1, hg)), head_of(MLA_NOPE), head_of(MLA_ROPE))


def _mla_kv_kernel(ckv_ref, kr_ref, gkv_ref, wuk_ref, wuv_ref, gkn_ref, gkr_ref, cos_ref, sin_ref, swap_ref,
                   lat_ref, kpe_ref, rinv_ref, kc_ref, v_ref, lat_scr, kpe_scr, *, hg):
    j = pl.program_id(1)

    @pl.when(j == 0)
    def _():
        lat = _rms(ckv_ref[...], gkv_ref[...])
        lat_ref[...] = lat
        lat_scr[...] = lat.astype(BF16)
        xg = kr_ref[...] * gkr_ref[...]
        x1, x2, x3 = _split3(xg)
        sw = swap_ref[...]
        swapped = _dot(x1, sw) + _dot(x2, sw) + _dot(x3, sw)
        kpe = xg * cos_ref[...] + swapped * sin_ref[...]
        kpe_ref[...] = kpe
        kpe_scr[...] = kpe
        rinv_ref[...] = jnp.zeros_like(rinv_ref)

    a = lat_scr[...]
    kn = _dot(a, wuk_ref[...])
    v_ref[...] = _dot(a, wuv_ref[...]).astype(v_ref.dtype)
    kr = kr_ref[...]
    ssr = jnp.sum(kr * kr, axis=-1, keepdims=True)
    kpe = kpe_scr[...]
    lane = lax.broadcasted_iota(jnp.int32, rinv_ref.shape, 1)
    racc = rinv_ref[...]
    gkn = gkn_ref[...]
    for h in range(hg):
        ns = slice(h * MLA_NOPE, (h + 1) * MLA_NOPE)
        x = kn[:, ns]
        ri = lax.rsqrt((jnp.sum(x * x, axis=-1, keepdims=True) + ssr) / MLA_QK + EPS)
        base = h * MLA_QK_PAD
        kc_ref[:, base:base + MLA_NOPE] = (x * gkn * ri).astype(kc_ref.dtype)
        kc_ref[:, base + MLA_NOPE:base + MLA_QK] = (kpe * ri).astype(kc_ref.dtype)
        kc_ref[:, base + MLA_QK:base + MLA_QK_PAD] = jnp.zeros((x.shape[0], MLA_QK_PAD - MLA_QK), kc_ref.dtype)
        racc = jnp.where(lane == j * hg + h, ri, racc)
    rinv_ref[...] = racc


def _mla_kv(proj, k_r, kv_norm_g, w_uk2, w_uv2, g_k, cos_t, sin_t):
    m = proj.shape[0]
    hg = MLA_KV_HEADS_PER_STEP
    tm = _tile(m, ROW_TILE, SUBLANE_BF16)
    half = MLA_ROPE // 2
    idx = jnp.arange(MLA_ROPE)
    swap = (idx[:, None] == (idx[None, :] + half) % MLA_ROPE).astype(BF16)
    row = lambda w: pl.BlockSpec((tm, w), lambda i, j: (i, 0))
    vec = lambda w: pl.BlockSpec((1, w), lambda i, j: (0, 0))
    return pl.pallas_call(
        functools.partial(_mla_kv_kernel, hg=hg),
        grid=(m // tm, MLA_HEADS // hg),
        in_specs=[pl.BlockSpec((tm, MLA_KV_RANK), lambda i, j: (i, MLA_Q_RANK // MLA_KV_RANK)),
                  row(MLA_ROPE), vec(MLA_KV_RANK),
                  pl.BlockSpec((MLA_KV_RANK, hg * MLA_NOPE), lambda i, j: (0, j)),
                  pl.BlockSpec((MLA_KV_RANK, hg * MLA_V), lambda i, j: (0, j)),
                  vec(MLA_NOPE), vec(MLA_ROPE), row(MLA_ROPE), row(MLA_ROPE),
                  pl.BlockSpec((MLA_ROPE, MLA_ROPE), lambda i, j: (0, 0))],
        out_specs=[row(MLA_KV_RANK), row(MLA_ROPE), row(MLA_HEADS),
                   pl.BlockSpec((tm, hg * MLA_QK_PAD), lambda i, j: (i, j)),
                   pl.BlockSpec((tm, hg * MLA_V), lambda i, j: (i, j))],
        out_shape=[jax.ShapeDtypeStruct((m, MLA_KV_RANK), F32),
                   jax.ShapeDtypeStruct((m, MLA_ROPE), F32),
                   jax.ShapeDtypeStruct((m, MLA_HEADS), F32),
                   jax.ShapeDtypeStruct((m, MLA_HEADS * MLA_QK_PAD), BF16),
                   jax.ShapeDtypeStruct((m, MLA_HEADS * MLA_V), BF16)],
        scratch_shapes=[pltpu.VMEM((tm, MLA_KV_RANK), BF16), pltpu.VMEM((tm, MLA_ROPE), F32)],
        compiler_params=_cparams(("parallel", "arbitrary")),
        name="mla_kv",
    )(proj, k_r, kv_norm_g.reshape(1, -1), w_uk2, w_uv2, g_k[:MLA_NOPE].reshape(1, -1),
      g_k[MLA_NOPE:].reshape(1, -1), cos_t, sin_t, swap)


def _flash_kernel(q_ref, k_ref, v_ref, o_ref, *, tq, hp):
    i = pl.program_id(2)
    krow = lax.broadcasted_iota(jnp.int32, (tq, tq), 0)
    qcol = lax.broadcasted_iota(jnp.int32, (tq, tq), 1)
    heads = range(hp)
    csl = [slice(h * MLA_QK_PAD, (h + 1) * MLA_QK_PAD) for h in heads]
    vsl = [slice(h * MLA_V, (h + 1) * MLA_V) for h in heads]
    q = [q_ref[:, csl[h]] for h in heads]

    def block(jb, carry, on_diagonal):
        ks = pl.ds(pl.multiple_of(jb * tq, tq), tq)
        s = [_dot_nt(k_ref[ks, csl[h]], q[h]) for h in heads]
        if on_diagonal:
            s = [jnp.where(krow <= qcol, s[h], -jnp.inf) for h in heads]
        m_new = [jnp.maximum(carry[h][0], jnp.max(s[h], axis=0, keepdims=True)) for h in heads]
        corr = [jnp.exp2(carry[h][0] - m_new[h]) for h in heads]
        p = [jnp.exp2(s[h] - m_new[h]) for h in heads]
        l = [carry[h][1] * corr[h] + jnp.sum(p[h], axis=0, keepdims=True) for h in heads]
        acc = [carry[h][2] * corr[h] + _dot_tn(v_ref[ks, vsl[h]], p[h]) for h in heads]
        return tuple((m_new[h], l[h], acc[h]) for h in heads)

    init = tuple((jnp.full((1, tq), -jnp.inf, F32), jnp.zeros((1, tq), F32), jnp.zeros((MLA_V, tq), F32))
                 for _ in heads)
    carry = lax.fori_loop(0, i, lambda jb, c: block(jb, c, False), init)
    carry = block(i, carry, True)
    ri = lax.broadcasted_iota(jnp.int32, (MLA_V, MLA_V), 0)
    ci = lax.broadcasted_iota(jnp.int32, (MLA_V, MLA_V), 1)
    eye = (ri == ci).astype(BF16)
    for h in heads:
        _, l, acc = carry[h]
        o_ref[:, vsl[h]] = _dot_tn((acc / l).astype(BF16), eye).astype(o_ref.dtype)


def _mla_prompt_attention(qc, kc, v, nb, seq):
    hp = FLASH_HEADS_PER_STEP
    tq = _tile(seq, FLASH_BLOCK, SUBLANE_BF16)
    nq = seq // tq
    qmap = lambda b, h, i: (b * nq + i, h)
    kmap = lambda b, h, i: (b, h)
    return pl.pallas_call(
        functools.partial(_flash_kernel, tq=tq, hp=hp),
        grid=(nb, MLA_HEADS // hp, nq),
        in_specs=[pl.BlockSpec((tq, hp * MLA_QK_PAD), qmap),
                  pl.BlockSpec((seq, hp * MLA_QK_PAD), kmap),
                  pl.BlockSpec((seq, hp * MLA_V), kmap)],
        out_specs=pl.BlockSpec((tq, hp * MLA_V), qmap),
        out_shape=jax.ShapeDtypeStruct((nb * seq, MLA_HEADS * MLA_V), BF16),
        compiler_params=_cparams(("parallel", "parallel", "arbitrary")),
        name="mla_prompt_attention",
    )(qc, kc, v)


def _absorb_q_kernel(q_ref, g_ref, w_ref, o_ref):
    o_ref[...] = _dot_nt(q_ref[...].astype(F32) * g_ref[...], w_ref[...]).astype(o_ref.dtype)


def _absorb_q(qc, g_k_nope, w_uk2, mp, ms):
    assert mp % ms == 0
    return pl.pallas_call(
        _absorb_q_kernel,
        grid=(MLA_HEADS,),
        in_specs=[pl.BlockSpec((ms, MLA_NOPE), lambda h: (mp // ms, h * (MLA_QK_PAD // MLA_NOPE))),
                  pl.BlockSpec((1, MLA_NOPE), lambda h: (0, 0)),
                  pl.BlockSpec((MLA_KV_RANK, MLA_NOPE), lambda h: (0, h))],
        out_specs=pl.BlockSpec((ms, MLA_KV_RANK), lambda h: (0, h)),
        out_shape=jax.ShapeDtypeStruct((ms, MLA_HEADS * MLA_KV_RANK), BF16),
        compiler_params=_cparams(("parallel",)),
        name="mla_absorb_q",
    )(qc, g_k_nope.reshape(1, -1), w_uk2)


def _expand_o_kernel(o_ref, w_ref, out_ref):
    out_ref[...] = _dot(o_ref[...], w_ref[...]).astype(out_ref.dtype)


def _expand_o(o_lat, w_uv2):
    ms = o_lat.shape[0]
    return pl.pallas_call(
        _expand_o_kernel,
        grid=(MLA_HEADS,),
        in_specs=[pl.BlockSpec((ms, MLA_KV_RANK), lambda h: (0, h)),
                  pl.BlockSpec((MLA_KV_RANK, MLA_V), lambda h: (0, h))],
        out_specs=pl.BlockSpec((ms, MLA_V), lambda h: (0, h)),
        out_shape=jax.ShapeDtypeStruct((ms, MLA_HEADS * MLA_V), BF16),
        compiler_params=_cparams(("parallel",)),
        name="mla_expand_o",
    )(o_lat, w_uv2)


def _softmax_update(sc, values, m_scr, l_scr, acc_scr):
    m_old = m_scr[...]
    m_new = jnp.maximum(m_old, jnp.max(sc, axis=-1, keepdims=True))
    corr = jnp.exp2(m_old - m_new)
    p = jnp.exp2(sc - m_new)
    l_scr[...] = l_scr[...] * corr + jnp.sum(p, axis=-1, keepdims=True)
    acc_scr[...] = acc_scr[...] * corr + _dot(p, values)
    m_scr[...] = m_new


def _sample_attn_kernel(pt_ref, ql_ref, qp_ref, *refs, pps, nsteps, steps, page):
    lat_refs = refs[:pps]
    pe_refs = refs[pps:2 * pps]
    ri_refs = refs[2 * pps:3 * pps]
    latn_ref, pen_ref, rin_ref, o_ref, m_scr, l_scr, acc_scr, lat_buf, pe_buf, ri_buf = refs[3 * pps:]
    s = pl.program_id(1)

    @pl.when(s == 0)
    def _():
        m_scr[...] = jnp.full_like(m_scr, -jnp.inf)
        l_scr[...] = jnp.zeros_like(l_scr)
        acc_scr[...] = jnp.zeros_like(acc_scr)

    for p in range(pps):
        keys = slice(p * page, (p + 1) * page)
        lat_buf[keys, :] = lat_refs[p][...].astype(BF16)
        pe_buf[:, keys] = pe_refs[p][...].astype(BF16)
        ri_buf[:, keys] = ri_refs[p][...]
    ql = ql_ref[...]
    qp = qp_ref[...]
    nsub = SAMPLE_SUBBLOCKS if pps % SAMPLE_SUBBLOCKS == 0 else 1
    width = pps * page // nsub
    subs = [slice(u * width, (u + 1) * width) for u in range(nsub)]
    lat = [lat_buf[sl, :] for sl in subs]
    sc = [(_dot_nt(ql, lat[u]) + _dot(qp, pe_buf[:, subs[u]])) * jnp.concatenate([ri_buf[:, subs[u]]] * steps, axis=0)
          for u in range(nsub)]
    for u in range(nsub):
        _softmax_update(sc[u], lat[u], m_scr, l_scr, acc_scr)

    @pl.when(s == nsteps - 1)
    def _():
        latn = latn_ref[...].astype(BF16)
        scn = (_dot_nt(ql, latn) + _dot(qp, pen_ref[...])) * jnp.concatenate([rin_ref[...]] * steps, axis=0)
        qstep = lax.broadcasted_iota(jnp.int32, scn.shape, 0) // MLA_HEADS
        kstep = lax.broadcasted_iota(jnp.int32, scn.shape, 1)
        scn = jnp.where(jnp.logical_and(kstep < steps, kstep <= qstep), scn, -jnp.inf)
        _softmax_update(scn, latn, m_scr, l_scr, acc_scr)
        o_ref[...] = (acc_scr[...] / l_scr[...]).astype(o_ref.dtype)


def _mla_sample_attention(q_lat, q_pe, lat_new, pe_new_t, rinv_new_t, cache_lat, cache_pe_t, cache_rinv_t, li,
                          page_table, steps):
    nb, n_pages = page_table.shape
    page = cache_lat.shape[2]
    pps = _tile(n_pages, PAGES_PER_STEP, 1)
    nsteps = n_pages // pps
    rows = q_lat.shape[1]
    npad = lat_new.shape[1]

    def page_spec(shape, p):
        return pl.BlockSpec((None, None) + shape, lambda b, s, pt: (li, pt[b * n_pages + s * pps + p], 0, 0))

    per_b = lambda r, w: pl.BlockSpec((None, r, w), lambda b, s, pt: (b, 0, 0))
    in_specs = ([per_b(rows, MLA_KV_RANK), per_b(rows, MLA_ROPE)]
                + [page_spec((page, MLA_KV_RANK), p) for p in range(pps)]
                + [page_spec((MLA_ROPE, page), p) for p in range(pps)]
                + [page_spec((MLA_HEADS, page), p) for p in range(pps)]
                + [per_b(npad, MLA_KV_RANK), per_b(MLA_ROPE, npad), per_b(MLA_HEADS, npad)])
    grid_spec = pltpu.PrefetchScalarGridSpec(
        num_scalar_prefetch=1,
        grid=(nb, nsteps),
        in_specs=in_specs,
        out_specs=per_b(rows, MLA_KV_RANK),
        scratch_shapes=[pltpu.VMEM((rows, 1), F32), pltpu.VMEM((rows, 1), F32),
                        pltpu.VMEM((rows, MLA_KV_RANK), F32),
                        pltpu.VMEM((pps * page, MLA_KV_RANK), BF16),
                        pltpu.VMEM((MLA_ROPE, pps * page), BF16),
                        pltpu.VMEM((MLA_HEADS, pps * page), F32)])
    return pl.pallas_call(
        functools.partial(_sample_attn_kernel, pps=pps, nsteps=nsteps, steps=steps, page=page),
        grid_spec=grid_spec,
        out_shape=jax.ShapeDtypeStruct((nb, rows, MLA_KV_RANK), BF16),
        compiler_params=_cparams(("parallel", "arbitrary")),
        name="mla_sample_attention",
    )(page_table.reshape(-1), q_lat, q_pe, *([cache_lat] * pps), *([cache_pe_t] * pps), *([cache_rinv_t] * pps),
      lat_new, pe_new_t.astype(BF16), rinv_new_t)


def _rope_tables(pos):
    half = MLA_ROPE // 2
    inv = ROPE_THETA ** (-jnp.arange(half, dtype=F32) / half)
    ang = pos.astype(F32)[:, None] * inv[None, :]
    cos, sin = jnp.cos(ang), jnp.sin(ang)
    return jnp.concatenate([cos, cos], axis=-1), jnp.concatenate([-sin, sin], axis=-1)


def _mla_mixer(h, x_res, cache_lat, cache_pe, cache_rinv, page_table, w_in, q_norm_g, kv_norm_g, w_uq, w_uk,
               w_uv, g_q, g_k, w_o, li, dims, past_len):
    mp, ms, nbp, seq, nbs, steps = dims
    pos = jnp.concatenate([jnp.tile(jnp.arange(seq), nbp), jnp.tile(past_len + jnp.arange(steps), nbs)])
    cos_t, sin_t = _rope_tables(pos)
    proj = _mm(h, w_in, (li,), name="mla_in_proj")
    w_uq3 = w_uq[li].reshape(MLA_Q_RANK, MLA_HEADS, MLA_QK)
    qc, qr = _mla_q(proj, q_norm_g[li], w_uq3[:, :, :MLA_NOPE].reshape(MLA_Q_RANK, -1),
                    w_uq3[:, :, MLA_NOPE:].reshape(MLA_Q_RANK, -1), g_q[li], cos_t, sin_t)
    w_uk2 = w_uk[li].reshape(MLA_KV_RANK, -1)
    w_uv2 = w_uv[li].reshape(MLA_KV_RANK, -1)
    lat, kpe, rinv, kc, v = _mla_kv(proj, proj[:, MLA_Q_RANK + MLA_KV_RANK:], kv_norm_g[li], w_uk2, w_uv2,
                                         g_k[li], cos_t, sin_t)
    o_p = _mla_prompt_attention(qc, kc, v, nbp, seq)

    q_lat = _absorb_q(qc, g_k[li, :MLA_NOPE], w_uk2, mp, ms)
    padn = lambda a: jnp.pad(a[mp:].reshape(nbs, steps, -1), ((0, 0), (0, 8 - steps), (0, 0)))
    o_lat = _mla_sample_attention(q_lat.reshape(nbs, steps * MLA_HEADS, MLA_KV_RANK),
                                  qr[mp:].reshape(nbs, steps * MLA_HEADS, MLA_ROPE),
                                  padn(lat), padn(kpe).swapaxes(1, 2), padn(rinv).swapaxes(1, 2), cache_lat,
                                  cache_pe.swapaxes(2, 3), cache_rinv.swapaxes(2, 3), li, page_table, steps)
    o_s = _expand_o(o_lat.reshape(ms, MLA_HEADS * MLA_KV_RANK), w_uv2)
    x_new = _mm(jnp.concatenate([o_p, o_s], axis=0), w_o, (li,), residual=x_res, tk=2048, name="mla_out_proj")
    return x_new, lat, kpe, rinv


def kernel(x_prompt, x_sample, state_gdn_ssm, state_gdn_conv, cache_mla_latent, cache_mla_k_pe, cache_mla_k_rinv,
           page_table, norm_g, ffn_w_gate, ffn_w_up, ffn_w_down, gdn_w_in, gdn_conv_w, gdn_a_log, gdn_dt_bias,
           gdn_norm_g, gdn_w_out, mla_w_in, mla_q_norm_g, mla_kv_norm_g, mla_w_uq, mla_w_uk, mla_w_uv,
           mla_qk_norm_q, mla_qk_norm_k, mla_w_o):
    nbp, seq, d = x_prompt.shape
    nbs, steps, _ = x_sample.shape
    mp, ms = nbp * seq, nbs * steps
    dims = (mp, ms, nbp, seq, nbs, steps)
    depth = norm_g.shape[0]
    past_len = page_table.shape[1] * cache_mla_latent.shape[2]
    x = jnp.concatenate([x_prompt.reshape(mp, d), x_sample.reshape(ms, d)], axis=0)
    w_down_bf16 = ffn_w_down.astype(BF16)
    h = _norm(x, norm_g[0, 0], BF16)
    outs = {k: [] for k in ("ssm_p", "conv_p", "ssm_s", "conv_s", "lat", "pe", "rinv")}
    for layer in range(depth):
        li = layer // 2
        x = _macaron_half(x, h, ffn_w_gate, ffn_w_up, w_down_bf16, layer, 0)
        h = _norm(x, norm_g[layer, 1], BF16)
        if layer % 2 == 0:
            x, ssm_p, conv_p, ssm_s, conv_s = _gdn_mixer(
                h, x, state_gdn_ssm, state_gdn_conv, gdn_w_in, gdn_conv_w, gdn_a_log, gdn_dt_bias, gdn_norm_g,
                gdn_w_out, li, dims)
            for key, val in (("ssm_p", ssm_p), ("conv_p", conv_p), ("ssm_s", ssm_s), ("conv_s", conv_s)):
                outs[key].append(val)
        else:
            x, lat, kpe, rinv = _mla_mixer(
                h, x, cache_mla_latent, cache_mla_k_pe, cache_mla_k_rinv, page_table, mla_w_in, mla_q_norm_g,
                mla_kv_norm_g, mla_w_uq, mla_w_uk, mla_w_uv, mla_qk_norm_q, mla_qk_norm_k, mla_w_o, li, dims,
                past_len)
            for key, val in (("lat", lat), ("pe", kpe), ("rinv", rinv)):
                outs[key].append(val)
        h = _norm(x, norm_g[layer, 2], BF16)
        x = _macaron_half(x, h, ffn_w_gate, ffn_w_up, w_down_bf16, layer, 1)
        if layer + 1 < depth:
            x, h = _norm2(x, norm_g[layer, 3], norm_g[layer + 1, 0])
        else:
            y_p = _norm(x, norm_g[layer, 3], F32, 0, mp).reshape(nbp, seq, d)
            y_s = _norm(x, norm_g[layer, 3], F32, mp, ms).reshape(nbs, steps, d)

    def split(a):
        return a[:mp].reshape(nbp, seq, -1), a[mp:].reshape(nbs, steps, -1)

    lat_p, lat_s = zip(*[split(a) for a in outs["lat"]])
    pe_p, pe_s = zip(*[split(a) for a in outs["pe"]])
    rinv_p, rinv_s = zip(*[split(a) for a in outs["rinv"]])
    return (y_p, y_s,
            jnp.stack(outs["ssm_p"]), jnp.stack(outs["conv_p"]), jnp.stack(lat_p), jnp.stack(pe_p),
            jnp.stack(rinv_p),
            jnp.stack(outs["ssm_s"]), jnp.stack(outs["conv_s"]), jnp.stack(lat_s), jnp.stack(pe_s),
            jnp.stack(rinv_s))
```

```python
import functools
import math

import jax
import jax.numpy as jnp
from jax import lax
from jax.experimental import pallas as pl
from jax.experimental.pallas import tpu as pltpu

F32 = jnp.float32
BF16 = jnp.bfloat16
EPS = 1e-6
ROPE_THETA = 10000.0

GDN_K_HEADS = 16
GDN_V_HEADS = 32
GDN_HEAD = 128
GDN_CONV = 4
GDN_CHUNK = 64
MLA_HEADS = 64
MLA_Q_RANK = 1024
MLA_KV_RANK = 512
MLA_NOPE = 128
MLA_ROPE = 64
MLA_V = 128
MLA_QK = MLA_NOPE + MLA_ROPE
MLA_QK_PAD = 256

LANE = 128
SUBLANE_BF16 = 16
VMEM_LIMIT = 56 * 1024 * 1024

FFN_DOWN_K_TILE = 3584
FFN_UP_ROW_TILE = 2176
ROW_TILE = 1088
GDN_BLOCK_ROWS = 256
GDN_GROUPS_PER_STEP = 8
MLA_Q_HEADS_PER_STEP = 8
MLA_KV_HEADS_PER_STEP = 8
FLASH_HEADS_PER_STEP = 8
FLASH_BLOCK = 512
PAGES_PER_STEP = 16
SAMPLE_SUBBLOCKS = 4
SAMPLE_PAD = 16


def _tile(n, target, align):
    if n <= target:
        return n
    t = (target // align) * align
    while t >= align:
        if n % t == 0:
            return t
        t -= align
    return n


def _cparams(sem):
    return pltpu.CompilerParams(dimension_semantics=sem, vmem_limit_bytes=VMEM_LIMIT)


def _dot(a, b):
    return jnp.dot(a.astype(BF16), b.astype(BF16), preferred_element_type=F32)


def _dot_nt(a, b):
    return lax.dot_general(a.astype(BF16), b.astype(BF16), (((1,), (1,)), ((), ())),
                           preferred_element_type=F32)


def _dot_tn(a, b):
    return lax.dot_general(a.astype(BF16), b.astype(BF16), (((0,), (0,)), ((), ())),
                           preferred_element_type=F32)


def _split3(x):
    x1 = x.astype(BF16)
    r1 = x - x1.astype(F32)
    x2 = r1.astype(BF16)
    x3 = (r1 - x2.astype(F32)).astype(BF16)
    return x1, x2, x3


def _rms(x, g):
    return x * lax.rsqrt(jnp.mean(x * x, axis=-1, keepdims=True) + EPS) * g


def _sigmoid(x):
    return 1.0 / (1.0 + jnp.exp(-x))


def _norm_kernel(x_ref, g_ref, o_ref):
    o_ref[...] = _rms(x_ref[...], g_ref[...]).astype(o_ref.dtype)


def _norm2_kernel(x_ref, g1_ref, g2_ref, y_ref, h_ref):
    y = _rms(x_ref[...], g1_ref[...])
    y_ref[...] = y
    h_ref[...] = _rms(y, g2_ref[...]).astype(h_ref.dtype)


def _norm(x, g, out_dtype, row0=0, nrows=None):
    d = x.shape[1]
    nrows = x.shape[0] if nrows is None else nrows
    tm = _tile(math.gcd(row0, nrows), 512, SUBLANE_BF16)
    return pl.pallas_call(
        _norm_kernel,
        grid=(nrows // tm,),
        in_specs=[pl.BlockSpec((tm, d), lambda i: (i + row0 // tm, 0)), pl.BlockSpec((1, d), lambda i: (0, 0))],
        out_specs=pl.BlockSpec((tm, d), lambda i: (i, 0)),
        out_shape=jax.ShapeDtypeStruct((nrows, d), out_dtype),
        compiler_params=_cparams(("parallel",)),
        name="rmsnorm",
    )(x, g.reshape(1, d))


def _norm2(x, g1, g2):
    m, d = x.shape
    tm = _tile(m, 512, SUBLANE_BF16)
    return pl.pallas_call(
        _norm2_kernel,
        grid=(m // tm,),
        in_specs=[pl.BlockSpec((tm, d), lambda i: (i, 0)), pl.BlockSpec((1, d), lambda i: (0, 0)),
                  pl.BlockSpec((1, d), lambda i: (0, 0))],
        out_specs=[pl.BlockSpec((tm, d), lambda i: (i, 0)), pl.BlockSpec((tm, d), lambda i: (i, 0))],
        out_shape=[jax.ShapeDtypeStruct((m, d), F32), jax.ShapeDtypeStruct((m, d), BF16)],
        compiler_params=_cparams(("parallel",)),
        name="rmsnorm_pair",
    )(x, g1.reshape(1, d), g2.reshape(1, d))


def _mm_kernel(*refs, has_res, nk, w_is_nk):
    if has_res:
        a_ref, w_ref, r_ref, o_ref = refs
    else:
        a_ref, w_ref, o_ref = refs
    part = (_dot_nt if w_is_nk else _dot)(a_ref[...], w_ref[...])
    if nk == 1:
        if has_res:
            part = part + r_ref[...]
        o_ref[...] = part.astype(o_ref.dtype)
        return
    k = pl.program_id(2)

    @pl.when(k == 0)
    def _():
        o_ref[...] = part + r_ref[...] if has_res else part

    @pl.when(k > 0)
    def _():
        o_ref[...] += part


def _mm(a, w, wprefix=(), *, n_out=None, out_dtype=F32, residual=None, tm=ROW_TILE, tn=1024, tk=1024,
        w_is_nk=False, name="matmul"):
    m, kdim = a.shape
    assert w.shape[-1 if w_is_nk else -2] == kdim
    n = n_out or w.shape[-2 if w_is_nk else -1]
    tm = _tile(m, tm, SUBLANE_BF16)
    tn = _tile(n, tn, LANE)
    tk = _tile(kdim, tk, LANE)
    nk = kdim // tk
    assert nk == 1 or out_dtype == F32
    npre = len(wprefix)
    in_specs = [pl.BlockSpec((tm, tk), lambda i, j, k: (i, k)),
                pl.BlockSpec((None,) * npre + (tn, tk), lambda i, j, k: tuple(wprefix) + (j, k)) if w_is_nk else
                pl.BlockSpec((None,) * npre + (tk, tn), lambda i, j, k: tuple(wprefix) + (k, j))]
    args = [a, w]
    if residual is not None:
        in_specs.append(pl.BlockSpec((tm, tn), lambda i, j, k: (i, j)))
        args.append(residual)
    return pl.pallas_call(
        functools.partial(_mm_kernel, has_res=residual is not None, nk=nk, w_is_nk=w_is_nk),
        grid=(m // tm, n // tn, nk),
        in_specs=in_specs,
        out_specs=pl.BlockSpec((tm, tn), lambda i, j, k: (i, j)),
        out_shape=jax.ShapeDtypeStruct((m, n), out_dtype),
        compiler_params=_cparams(("parallel", "parallel", "arbitrary")),
        name=name,
    )(*args)


def _ffn_up_kernel(h_ref, wg_ref, wu_ref, o_ref):
    h = h_ref[...]
    gate = _dot(h, wg_ref[...])
    up = _dot(h, wu_ref[...])
    o_ref[...] = (0.5 * gate * _sigmoid(gate) * up).astype(o_ref.dtype)


def _ffn_up(h, w_gate, w_up, layer, half):
    m, d = h.shape
    f_dim = w_gate.shape[-1]
    tm = _tile(m, FFN_UP_ROW_TILE, SUBLANE_BF16)
    tf = _tile(f_dim, 256, LANE)
    wspec = pl.BlockSpec((None, None, d, tf), lambda i, f: (layer, half, 0, f))
    return pl.pallas_call(
        _ffn_up_kernel,
        grid=(m // tm, f_dim // tf),
        in_specs=[pl.BlockSpec((tm, d), lambda i, f: (i, 0), pipeline_mode=pl.Buffered(1)), wspec, wspec],
        out_specs=pl.BlockSpec((tm, tf), lambda i, f: (i, f)),
        out_shape=jax.ShapeDtypeStruct((m, f_dim), BF16),
        compiler_params=_cparams(("parallel", "arbitrary")),
        name="ffn_up",
    )(h, w_gate, w_up)


def _split_contraction(kdim, target):
    tk = _tile(kdim, target, LANE)
    if kdim // tk <= 8:
        return tk, kdim // tk, 0
    for tail in (2 * LANE, LANE):
        main = kdim - tail
        tk = _tile(main, target, LANE)
        if main % tail == 0 and main // tk <= 8:
            return tk, main // tk, tail
    raise ValueError(f"no contraction tiling for {kdim}")


def _ffn_down_kernel(a_ref, w_ref, at_ref, wt_ref, r_ref, o_ref):
    k = pl.program_id(2)
    part = _dot(a_ref[...], w_ref[...])

    @pl.when(k == 0)
    def _():
        o_ref[...] = r_ref[...] + part + _dot(at_ref[...], wt_ref[...])

    @pl.when(k > 0)
    def _():
        o_ref[...] += part


def _macaron_half(x, h, w_gate, w_up, w_down_bf16, layer, half):
    act = _ffn_up(h, w_gate, w_up, layer, half)
    m, f_dim = act.shape
    n = w_down_bf16.shape[-1]
    tk, nk, tail = _split_contraction(f_dim, FFN_DOWN_K_TILE)
    if tail == 0:
        return _mm(act, w_down_bf16, (layer, half), residual=x, tk=tk, name="ffn_down")
    tm = _tile(m, ROW_TILE, SUBLANE_BF16)
    tn = _tile(n, 1024, LANE)
    tail_blk = nk * tk // tail
    return pl.pallas_call(
        _ffn_down_kernel,
        grid=(m // tm, n // tn, nk),
        in_specs=[pl.BlockSpec((tm, tk), lambda i, j, k: (i, k)),
                  pl.BlockSpec((None, None, tk, tn), lambda i, j, k: (layer, half, k, j)),
                  pl.BlockSpec((tm, tail), lambda i, j, k: (i, tail_blk)),
                  pl.BlockSpec((None, None, tail, tn), lambda i, j, k: (layer, half, tail_blk, j)),
                  pl.BlockSpec((tm, tn), lambda i, j, k: (i, j))],
        out_specs=pl.BlockSpec((tm, tn), lambda i, j, k: (i, j)),
        out_shape=jax.ShapeDtypeStruct((m, n), F32),
        compiler_params=_cparams(("parallel", "parallel", "arbitrary")),
        name="ffn_down",
    )(act, w_down_bf16, act, w_down_bf16, x)


def _gates_kernel(b_ref, a_ref, alog_ref, dt_ref, beta_ref, g_ref):
    beta_ref[...] = _sigmoid(b_ref[...])
    x = a_ref[...] + dt_ref[...]
    softplus = jnp.maximum(x, 0.0) + jnp.log1p(jnp.exp(-jnp.abs(x)))
    g_ref[...] = -jnp.exp(alog_ref[...]) * softplus


def _gdn_gates(beta_in, a_in, a_log, dt_bias):
    m, h = beta_in.shape
    tm = _tile(m, 2176, 8)
    row = pl.BlockSpec((tm, h), lambda i: (i, 0))
    vec = pl.BlockSpec((1, h), lambda i: (0, 0))
    return pl.pallas_call(
        _gates_kernel,
        grid=(m // tm,),
        in_specs=[row, row, vec, vec],
        out_specs=[row, row],
        out_shape=[jax.ShapeDtypeStruct((m, h), F32)] * 2,
        compiler_params=_cparams(("parallel",)),
        name="gdn_gates",
    )(beta_in, a_in, a_log.reshape(1, h), dt_bias.reshape(1, h))


def _conv_finish(y, o_ref, row_sl, is_qk, qk_scale):
    y = y * _sigmoid(y)
    tc = y.shape[-1]

    @pl.when(is_qk)
    def _():
        for h in range(tc // GDN_HEAD):
            sl = slice(h * GDN_HEAD, (h + 1) * GDN_HEAD)
            yh = y[:, sl]
            r = lax.rsqrt(jnp.sum(yh * yh, axis=-1, keepdims=True) + EPS)
            o_ref[row_sl + (sl,)] = (yh * (r * qk_scale)).astype(o_ref.dtype)

    @pl.when(jnp.logical_not(is_qk))
    def _():
        o_ref[row_sl + (slice(None),)] = y.astype(o_ref.dtype)


def _conv_prompt_kernel(x_ref, halo_ref, w_ref, o_ref, *, tiles_per_seq, nq_tiles, nqk_tiles):
    i = pl.program_id(0)
    c = pl.program_id(1)
    x = x_ref[...]
    tt = x.shape[0]
    halo = jnp.where(i % tiles_per_seq == 0, 0.0, halo_ref[...])
    xp = jnp.concatenate([halo, x], axis=0)
    w = w_ref[...]
    hl = halo.shape[0]
    y = pltpu.roll(xp, GDN_CONV - 1, 0)[hl:] * w[0:1]
    for j in range(1, GDN_CONV - 1):
        y = y + pltpu.roll(xp, GDN_CONV - 1 - j, 0)[hl:] * w[j:j + 1]
    y = y + x * w[GDN_CONV - 1:GDN_CONV]
    qk_scale = jnp.where(c < nq_tiles, GDN_HEAD ** -0.5, 1.0)
    _conv_finish(y, o_ref, (slice(None),), c < nqk_tiles, qk_scale)


def _conv_sample_kernel(x_ref, w_ref, o_ref, *, steps, nq_tiles, nqk_tiles):
    c = pl.program_id(0)
    w = w_ref[...]
    qk_scale = jnp.where(c < nq_tiles, GDN_HEAD ** -0.5, 1.0)
    for s in range(steps):
        y = x_ref[s] * w[0:1]
        for j in range(1, GDN_CONV):
            y = y + x_ref[s + j] * w[j:j + 1]
        _conv_finish(y, o_ref, (s, slice(None)), c < nqk_tiles, qk_scale)


def _gdn_conv_prompt(qkvz, conv_w, mp, seq):
    cdim = conv_w.shape[-1]
    tt = _tile(seq, 256, SUBLANE_BF16)
    tc = _tile(2 * GDN_K_HEADS * GDN_HEAD // 2, 512, GDN_HEAD)
    nq = GDN_K_HEADS * GDN_HEAD // tc
    return pl.pallas_call(
        functools.partial(_conv_prompt_kernel, tiles_per_seq=seq // tt, nq_tiles=nq, nqk_tiles=2 * nq),
        grid=(mp // tt, cdim // tc),
        in_specs=[pl.BlockSpec((tt, tc), lambda i, c: (i, c)),
                  pl.BlockSpec((SUBLANE_BF16, tc), lambda i, c: (jnp.maximum(i * (tt // SUBLANE_BF16) - 1, 0), c)),
                  pl.BlockSpec((GDN_CONV, tc), lambda i, c: (0, c))],
        out_specs=pl.BlockSpec((tt, tc), lambda i, c: (i, c)),
        out_shape=jax.ShapeDtypeStruct((mp, cdim), BF16),
        compiler_params=_cparams(("parallel", "parallel")),
        name="gdn_conv_prompt",
    )(qkvz, qkvz, conv_w)


def _gdn_conv_sample(xpad_t, conv_w, steps):
    tp, nb, cdim = xpad_t.shape
    tc = _tile(GDN_K_HEADS * GDN_HEAD, 512, GDN_HEAD)
    nq = GDN_K_HEADS * GDN_HEAD // tc
    return pl.pallas_call(
        functools.partial(_conv_sample_kernel, steps=steps, nq_tiles=nq, nqk_tiles=2 * nq),
        grid=(cdim // tc,),
        in_specs=[pl.BlockSpec((tp, nb, tc), lambda c: (0, 0, c)),
                  pl.BlockSpec((GDN_CONV, tc), lambda c: (0, c))],
        out_specs=pl.BlockSpec((steps, nb, tc), lambda c: (0, 0, c)),
        out_shape=jax.ShapeDtypeStruct((steps, nb, cdim), BF16),
        compiler_params=_cparams(("parallel",)),
        name="gdn_conv_sample",
    )(xpad_t, conv_w)


def _gdn_chunk_kernel(*refs, c, hb, ngroups, rep, nchunks, has_s0):
    if has_s0:
        q_ref, k_ref, v_ref, z_ref, g_ref, b_ref, ng_ref, s0_ref, o_ref, sfin_ref, s_scr = refs
    else:
        q_ref, k_ref, v_ref, z_ref, g_ref, b_ref, ng_ref, o_ref, sfin_ref, s_scr = refs
    n = pl.program_id(2)

    @pl.when(n == 0)
    def _():
        s_scr[...] = s0_ref[...] if has_s0 else jnp.zeros_like(s_scr)

    d = GDN_HEAD
    r = hb * c
    rowi = lax.broadcasted_iota(jnp.int32, (r, r), 0)
    coli = lax.broadcasted_iota(jnp.int32, (r, r), 1)
    same = (rowi ^ coli) < c
    tril = jnp.logical_and(same, rowi >= coli)
    strict = jnp.logical_and(same, rowi > coli)
    triu = jnp.logical_and(same, rowi <= coli)
    eye = rowi == coli
    eye_f = eye.astype(F32)
    head_of_row = lax.broadcasted_iota(jnp.int32, (r, hb), 0) // c == lax.broadcasted_iota(jnp.int32, (r, hb), 1)
    ng = ng_ref[...]

    def stack(ref, heads):
        return jnp.concatenate([ref[:, h * d:(h + 1) * d] for h in heads], axis=0)

    def stacked_column(ref, j0):
        tiled = jnp.concatenate([ref[:, j0:j0 + hb]] * hb, axis=0)
        return jnp.sum(jnp.where(head_of_row, tiled, 0.0), axis=1, keepdims=True)

    groups = range(ngroups)
    vheads = [list(range(gi * hb, (gi + 1) * hb)) for gi in groups]
    qst = [stack(q_ref, [h // rep for h in vheads[gi]]) for gi in groups]
    kst = [stack(k_ref, [h // rep for h in vheads[gi]]) for gi in groups]
    kf = [kst[gi].astype(F32) for gi in groups]
    vf = [stack(v_ref, vheads[gi]).astype(F32) for gi in groups]
    g = [stacked_column(g_ref, gi * hb) for gi in groups]
    beta = [stacked_column(b_ref, gi * hb) for gi in groups]
    qk_kk = [_dot_nt(jnp.concatenate([qst[gi], kst[gi]], axis=0), kst[gi]) for gi in groups]
    g_row = [jnp.sum(jnp.where(eye, g[gi], 0.0), axis=0, keepdims=True) for gi in groups]
    gc_col = [jnp.sum(jnp.where(tril, g_row[gi], 0.0), axis=1, keepdims=True) for gi in groups]
    gc_row = [jnp.sum(jnp.where(triu, g[gi], 0.0), axis=0, keepdims=True) for gi in groups]
    decay = [jnp.where(tril, jnp.exp(jnp.where(tril, gc_col[gi] - gc_row[gi], 0.0)), 0.0) for gi in groups]
    qk = [qk_kk[gi][:r] * decay[gi] for gi in groups]
    mm = [-jnp.where(strict, qk_kk[gi][r:] * beta[gi] * decay[gi], 0.0) for gi in groups]
    inv = [eye_f + mm[gi] for gi in groups]
    for _ in range(int(math.log2(c)) - 1):
        mm = [_dot(mm[gi], mm[gi]) for gi in groups]
        inv = [inv[gi] + _dot(inv[gi], mm[gi]) for gi in groups]
    e_col = [jnp.exp(gc_col[gi]) for gi in groups]
    uw = [_dot(inv[gi], jnp.concatenate([vf[gi] * beta[gi], kf[gi] * (beta[gi] * e_col[gi])], axis=1))
          for gi in groups]
    qd = [qst[gi].astype(F32) * e_col[gi] for gi in groups]
    heads = [(gi, j) for gi in groups for j in range(hb)]
    rows = [slice(j * c, (j + 1) * c) for j in range(hb)]
    s_old = {(gi, j): s_scr[gi * hb + j] for gi, j in heads}
    wq = {(gi, j): _dot(jnp.concatenate([uw[gi][rows[j], d:], qd[gi][rows[j]]], axis=0), s_old[gi, j])
          for gi, j in heads}
    v_new = {(gi, j): uw[gi][rows[j], :d] - wq[gi, j][:c] for gi, j in heads}
    for gi, j in heads:
        gc_last = gc_col[gi][(j + 1) * c - 1:(j + 1) * c, :]
        k_dec = kf[gi][rows[j]] * jnp.exp(gc_last - gc_col[gi][rows[j]])
        s_scr[gi * hb + j] = s_old[gi, j] * jnp.exp(gc_last) + _dot_tn(k_dec, v_new[gi, j])
    o = [jnp.concatenate([wq[gi, j][c:] for j in range(hb)], axis=0)
         + _dot(qk[gi], jnp.concatenate([v_new[gi, j] for j in range(hb)], axis=0)) for gi in groups]
    for gi in groups:
        z = stack(z_ref, vheads[gi])
        out = (_rms(o[gi], ng) * (z * _sigmoid(z))).astype(o_ref.dtype)
        for j in range(hb):
            o_ref[:, (gi * hb + j) * d:(gi * hb + j + 1) * d] = out[rows[j]]

    @pl.when(n == nchunks - 1)
    def _():
        sfin_ref[...] = s_scr[...]


def _gdn_rule(qkv, z, z_col0, g, beta, norm_g, s0, nb, seq, c):
    d = GDN_HEAD
    hb = min(GDN_BLOCK_ROWS // c, GDN_V_HEADS)
    ngroups = min(GDN_GROUPS_PER_STEP, GDN_V_HEADS // hb)
    hs = hb * ngroups
    rep = GDN_V_HEADS // GDN_K_HEADS
    nchunks = seq // c
    koff = GDN_K_HEADS // (hs // rep)
    voff = 2 * GDN_K_HEADS // hs
    zoff = z_col0 // (hs * d)
    rowmap = lambda b, h, n: (b * nchunks + n, h)
    gatemap = lambda b, h, n: (h, b * nchunks + n, 0)
    in_specs = [pl.BlockSpec((c, hs // rep * d), rowmap),
                pl.BlockSpec((c, hs // rep * d), lambda b, h, n: (b * nchunks + n, koff + h)),
                pl.BlockSpec((c, hs * d), lambda b, h, n: (b * nchunks + n, voff + h)),
                pl.BlockSpec((c, hs * d), lambda b, h, n: (b * nchunks + n, zoff + h)),
                pl.BlockSpec((None, c, hs), gatemap),
                pl.BlockSpec((None, c, hs), gatemap),
                pl.BlockSpec((1, d), lambda b, h, n: (0, 0))]
    args = [qkv, qkv, qkv, z, _head_groups(g, hs), _head_groups(beta, hs), norm_g.reshape(1, d)]
    if s0 is not None:
        in_specs.append(pl.BlockSpec((None, hs, d, d), lambda b, h, n: (b, h, 0, 0)))
        args.append(s0)
    return pl.pallas_call(
        functools.partial(_gdn_chunk_kernel, c=c, hb=hb, ngroups=ngroups, rep=rep, nchunks=nchunks,
                          has_s0=s0 is not None),
        grid=(nb, GDN_V_HEADS // hs, nchunks),
        in_specs=in_specs,
        out_specs=[pl.BlockSpec((c, hs * d), rowmap),
                   pl.BlockSpec((None, hs, d, d), lambda b, h, n: (b, h, 0, 0))],
        out_shape=[jax.ShapeDtypeStruct((nb * seq, GDN_V_HEADS * d), BF16),
                   jax.ShapeDtypeStruct((nb, GDN_V_HEADS, d, d), F32)],
        scratch_shapes=[pltpu.VMEM((hs, d, d), F32)],
        compiler_params=_cparams(("parallel", "parallel", "arbitrary")),
        name="gdn_rule",
    )(*args)


def _head_groups(x, hg):
    r, h = x.shape
    return x.reshape(r, h // hg, hg).transpose(1, 0, 2)


def _pad_steps(x, nb, steps, pad_to):
    cdim = x.shape[-1]
    x = x.reshape(nb, steps, cdim)
    return jnp.pad(x, ((0, 0), (0, pad_to - steps), (0, 0))).reshape(nb * pad_to, cdim)


def _gdn_mixer(h, x_res, state_ssm, state_conv, w_in, conv_w, a_log, dt_bias, norm_g, w_out, li, dims):
    mp, ms, nbp, seq, nbs, steps = dims
    cdim = conv_w.shape[-1]
    vdim = GDN_V_HEADS * GDN_HEAD
    w_in_t = jnp.swapaxes(w_in, 1, 2)
    qkvz = _mm(h, w_in_t, (li,), n_out=cdim + vdim, tn=512, tk=h.shape[1], w_is_nk=True, name="gdn_in_proj")
    ba = _mm(h, w_in_t[li, cdim + vdim:], w_is_nk=True, name="gdn_gate_proj")
    beta, g = _gdn_gates(ba[:, :GDN_V_HEADS], ba[:, GDN_V_HEADS:], a_log[li], dt_bias[li])

    qkv_p = _gdn_conv_prompt(qkvz, conv_w[li], mp, seq)
    o_p, ssm_p = _gdn_rule(qkv_p, qkvz, cdim, g[:mp], beta[:mp], norm_g[li], None, nbp, seq,
                           GDN_CHUNK if seq % GDN_CHUNK == 0 else seq)
    conv_p = jnp.stack([qkvz[(b + 1) * seq - (GDN_CONV - 1):(b + 1) * seq, :cdim] for b in range(nbp)])

    qkv_s = qkvz[mp:, :cdim].reshape(nbs, steps, cdim)
    win = jnp.concatenate([state_conv[li], qkv_s], axis=1)
    conv_s = win[:, win.shape[1] - (GDN_CONV - 1):]
    win_t = jnp.pad(win, ((0, 0), (0, 8 - win.shape[1] % 8 if win.shape[1] % 8 else 0), (0, 0))).transpose(1, 0, 2)
    qkv_c = _gdn_conv_sample(win_t, conv_w[li], steps).transpose(1, 0, 2).reshape(ms, cdim)
    pad = lambda a: _pad_steps(a, nbs, steps, SAMPLE_PAD)
    o_s, ssm_s = _gdn_rule(pad(qkv_c), pad(qkvz[mp:, cdim:cdim + vdim]), 0, pad(g[mp:]), pad(beta[mp:]),
                           norm_g[li], state_ssm[li], nbs, SAMPLE_PAD, SAMPLE_PAD)
    o_s = o_s.reshape(nbs, SAMPLE_PAD, vdim)[:, :steps].reshape(ms, vdim)
    o_all = jnp.concatenate([o_p, o_s], axis=0)
    x_new = _mm(o_all, w_out, (li,), residual=x_res, tk=2048, name="gdn_out_proj")
    return x_new, ssm_p, conv_p, ssm_s, conv_s


def _mla_q_kernel(cq_ref, gq_ref, wn_ref, wr_ref, gn_ref, gr_ref, cos_ref, sin_ref, en_ref, er_ref, qc_ref, qr_ref,
                  cqn_scr, *, hg, scale):
    j = pl.program_id(1)

    @pl.when(j == 0)
    def _():
        cqn_scr[...] = _rms(cq_ref[...], gq_ref[...]).astype(BF16)

    a = cqn_scr[...]
    qn = _dot(a, wn_ref[...])
    qr = _dot(a, wr_ref[...])
    width = hg * MLA_ROPE
    half = MLA_ROPE // 2
    xg = qr * gr_ref[...]
    lane = lax.broadcasted_iota(jnp.int32, xg.shape, 1)
    swapped = jnp.where(lane % MLA_ROPE < half, pltpu.roll(xg, width - half, 1), pltpu.roll(xg, half, 1))
    roped = xg * cos_ref[...] + swapped * sin_ref[...]
    gn = gn_ref[...]
    ss = _dot(qn * qn, en_ref[...]) + _dot(qr * qr, er_ref[...])
    r_all = lax.rsqrt(ss / MLA_QK + EPS) * scale
    for h in range(hg):
        ns = slice(h * MLA_NOPE, (h + 1) * MLA_NOPE)
        rs = slice(h * MLA_ROPE, (h + 1) * MLA_ROPE)
        an = qn[:, ns]
        r = r_all[:, h:h + 1]
        q_rope = (roped[:, rs] * r).astype(qr_ref.dtype)
        base = h * MLA_QK_PAD
        qc_ref[:, base:base + MLA_NOPE] = (an * r * gn).astype(qc_ref.dtype)
        qc_ref[:, base + MLA_NOPE:base + MLA_QK] = q_rope
        qc_ref[:, base + MLA_QK:base + MLA_QK_PAD] = jnp.zeros((an.shape[0], MLA_QK_PAD - MLA_QK), qc_ref.dtype)
        qr_ref[:, rs] = q_rope


def _mla_q(proj, q_norm_g, w_uq_nope, w_uq_rope, g_q, cos_t, sin_t):
    m = proj.shape[0]
    hg = MLA_Q_HEADS_PER_STEP
    tm = _tile(m, ROW_TILE, SUBLANE_BF16)
    gr_t = jnp.tile(g_q[MLA_NOPE:], hg).reshape(1, hg * MLA_ROPE)
    head_of = lambda width: (jnp.arange(hg * width)[:, None] // width == jnp.arange(hg)[None, :]).astype(BF16)
    return pl.pallas_call(
        functools.partial(_mla_q_kernel, hg=hg, scale=MLA_QK ** -0.5 * math.log2(math.e)),
        grid=(m // tm, MLA_HEADS // hg),
        in_specs=[pl.BlockSpec((tm, MLA_Q_RANK), lambda i, j: (i, 0)),
                  pl.BlockSpec((1, MLA_Q_RANK), lambda i, j: (0, 0)),
                  pl.BlockSpec((MLA_Q_RANK, hg * MLA_NOPE), lambda i, j: (0, j)),
                  pl.BlockSpec((MLA_Q_RANK, hg * MLA_ROPE), lambda i, j: (0, j)),
                  pl.BlockSpec((1, MLA_NOPE), lambda i, j: (0, 0)),
                  pl.BlockSpec((1, hg * MLA_ROPE), lambda i, j: (0, 0)),
                  pl.BlockSpec((tm, hg * MLA_ROPE), lambda i, j: (i, 0)),
                  pl.BlockSpec((tm, hg * MLA_ROPE), lambda i, j: (i, 0)),
                  pl.BlockSpec((hg * MLA_NOPE, hg), lambda i, j: (0, 0)),
                  pl.BlockSpec((hg * MLA_ROPE, hg), lambda i, j: (0, 0))],
        out_specs=[pl.BlockSpec((tm, hg * MLA_QK_PAD), lambda i, j: (i, j)),
                   pl.BlockSpec((tm, hg * MLA_ROPE), lambda i, j: (i, j))],
        out_shape=[jax.ShapeDtypeStruct((m, MLA_HEADS * MLA_QK_PAD), BF16),
                   jax.ShapeDtypeStruct((m, MLA_HEADS * MLA_ROPE), BF16)],
        scratch_shapes=[pltpu.VMEM((tm, MLA_Q_RANK), BF16)],
        compiler_params=_cparams(("parallel", "arbitrary")),
        name="mla_q",
    )(proj, q_norm_g.reshape(1, -1), w_uq_nope, w_uq_rope, g_q[:MLA_NOPE].reshape(1, -1), gr_t,
      jnp.tile(cos_t, (1, hg)), jnp.tile(sin_t, (1, hg)), head_of(MLA_NOPE), head_of(MLA_ROPE))


def _mla_kv_kernel(ckv_ref, kr_ref, gkv_ref, wuk_ref, wuv_ref, gkn_ref, gkr_ref, cos_ref, sin_ref, swap_ref,
                   lat_ref, kpe_ref, rinv_ref, kc_ref, v_ref, lat_scr, kpe_scr, *, hg):
    j = pl.program_id(1)

    @pl.when(j == 0)
    def _():
        lat = _rms(ckv_ref[...], gkv_ref[...])
        lat_ref[...] = lat
        lat_scr[...] = lat.astype(BF16)
        xg = kr_ref[...] * gkr_ref[...]
        x1, x2, x3 = _split3(xg)
        sw = swap_ref[...]
        swapped = _dot(x1, sw) + _dot(x2, sw) + _dot(x3, sw)
        kpe = xg * cos_ref[...] + swapped * sin_ref[...]
        kpe_ref[...] = kpe
        kpe_scr[...] = kpe
        rinv_ref[...] = jnp.zeros_like(rinv_ref)

    a = lat_scr[...]
    kn = _dot(a, wuk_ref[...])
    v_ref[...] = _dot(a, wuv_ref[...]).astype(v_ref.dtype)
    kr = kr_ref[...]
    ssr = jnp.sum(kr * kr, axis=-1, keepdims=True)
    kpe = kpe_scr[...]
    lane = lax.broadcasted_iota(jnp.int32, rinv_ref.shape, 1)
    racc = rinv_ref[...]
    gkn = gkn_ref[...]
    for h in range(hg):
        ns = slice(h * MLA_NOPE, (h + 1) * MLA_NOPE)
        x = kn[:, ns]
        ri = lax.rsqrt((jnp.sum(x * x, axis=-1, keepdims=True) + ssr) / MLA_QK + EPS)
        base = h * MLA_QK_PAD
        kc_ref[:, base:base + MLA_NOPE] = (x * gkn * ri).astype(kc_ref.dtype)
        kc_ref[:, base + MLA_NOPE:base + MLA_QK] = (kpe * ri).astype(kc_ref.dtype)
        kc_ref[:, base + MLA_QK:base + MLA_QK_PAD] = jnp.zeros((x.shape[0], MLA_QK_PAD - MLA_QK), kc_ref.dtype)
        racc = jnp.where(lane == j * hg + h, ri, racc)
    rinv_ref[...] = racc


def _mla_kv(proj, k_r, kv_norm_g, w_uk2, w_uv2, g_k, cos_t, sin_t):
    m = proj.shape[0]
    hg = MLA_KV_HEADS_PER_STEP
    tm = _tile(m, ROW_TILE, SUBLANE_BF16)
    half = MLA_ROPE // 2
    idx = jnp.arange(MLA_ROPE)
    swap = (idx[:, None] == (idx[None, :] + half) % MLA_ROPE).astype(BF16)
    row = lambda w: pl.BlockSpec((tm, w), lambda i, j: (i, 0))
    vec = lambda w: pl.BlockSpec((1, w), lambda i, j: (0, 0))
    return pl.pallas_call(
        functools.partial(_mla_kv_kernel, hg=hg),
        grid=(m // tm, MLA_HEADS // hg),
        in_specs=[pl.BlockSpec((tm, MLA_KV_RANK), lambda i, j: (i, MLA_Q_RANK // MLA_KV_RANK)),
                  row(MLA_ROPE), vec(MLA_KV_RANK),
                  pl.BlockSpec((MLA_KV_RANK, hg * MLA_NOPE), lambda i, j: (0, j)),
                  pl.BlockSpec((MLA_KV_RANK, hg * MLA_V), lambda i, j: (0, j)),
                  vec(MLA_NOPE), vec(MLA_ROPE), row(MLA_ROPE), row(MLA_ROPE),
                  pl.BlockSpec((MLA_ROPE, MLA_ROPE), lambda i, j: (0, 0))],
        out_specs=[row(MLA_KV_RANK), row(MLA_ROPE), row(MLA_HEADS),
                   pl.BlockSpec((tm, hg * MLA_QK_PAD), lambda i, j: (i, j)),
                   pl.BlockSpec((tm, hg * MLA_V), lambda i, j: (i, j))],
        out_shape=[jax.ShapeDtypeStruct((m, MLA_KV_RANK), F32),
                   jax.ShapeDtypeStruct((m, MLA_ROPE), F32),
                   jax.ShapeDtypeStruct((m, MLA_HEADS), F32),
                   jax.ShapeDtypeStruct((m, MLA_HEADS * MLA_QK_PAD), BF16),
                   jax.ShapeDtypeStruct((m, MLA_HEADS * MLA_V), BF16)],
        scratch_shapes=[pltpu.VMEM((tm, MLA_KV_RANK), BF16), pltpu.VMEM((tm, MLA_ROPE), F32)],
        compiler_params=_cparams(("parallel", "arbitrary")),
        name="mla_kv",
    )(proj, k_r, kv_norm_g.reshape(1, -1), w_uk2, w_uv2, g_k[:MLA_NOPE].reshape(1, -1),
      g_k[MLA_NOPE:].reshape(1, -1), cos_t, sin_t, swap)


def _flash_kernel(q_ref, k_ref, v_ref, o_ref, *, tq, hp):
    i = pl.program_id(2)
    krow = lax.broadcasted_iota(jnp.int32, (tq, tq), 0)
    qcol = lax.broadcasted_iota(jnp.int32, (tq, tq), 1)
    heads = range(hp)
    csl = [slice(h * MLA_QK_PAD, (h + 1) * MLA_QK_PAD) for h in heads]
    vsl = [slice(h * MLA_V, (h + 1) * MLA_V) for h in heads]
    q = [q_ref[:, csl[h]] for h in heads]

    def block(jb, carry, on_diagonal):
        ks = pl.ds(pl.multiple_of(jb * tq, tq), tq)
        s = [_dot_nt(k_ref[ks, csl[h]], q[h]) for h in heads]
        if on_diagonal:
            s = [jnp.where(krow <= qcol, s[h], -jnp.inf) for h in heads]
        m_new = [jnp.maximum(carry[h][0], jnp.max(s[h], axis=0, keepdims=True)) for h in heads]
        corr = [jnp.exp2(carry[h][0] - m_new[h]) for h in heads]
        p = [jnp.exp2(s[h] - m_new[h]) for h in heads]
        l = [carry[h][1] * corr[h] + jnp.sum(p[h], axis=0, keepdims=True) for h in heads]
        acc = [carry[h][2] * corr[h] + _dot_tn(v_ref[ks, vsl[h]], p[h]) for h in heads]
        return tuple((m_new[h], l[h], acc[h]) for h in heads)

    init = tuple((jnp.full((1, tq), -jnp.inf, F32), jnp.zeros((1, tq), F32), jnp.zeros((MLA_V, tq), F32))
                 for _ in heads)
    carry = lax.fori_loop(0, i, lambda jb, c: block(jb, c, False), init)
    carry = block(i, carry, True)
    ri = lax.broadcasted_iota(jnp.int32, (MLA_V, MLA_V), 0)
    ci = lax.broadcasted_iota(jnp.int32, (MLA_V, MLA_V), 1)
    eye = (ri == ci).astype(BF16)
    for h in heads:
        _, l, acc = carry[h]
        o_ref[:, vsl[h]] = _dot_tn((acc / l).astype(BF16), eye).astype(o_ref.dtype)


def _mla_prompt_attention(qc, kc, v, nb, seq):
    hp = FLASH_HEADS_PER_STEP
    tq = _tile(seq, FLASH_BLOCK, SUBLANE_BF16)
    nq = seq // tq
    qmap = lambda b, h, i: (b * nq + i, h)
    kmap = lambda b, h, i: (b, h)
    return pl.pallas_call(
        functools.partial(_flash_kernel, tq=tq, hp=hp),
        grid=(nb, MLA_HEADS // hp, nq),
        in_specs=[pl.BlockSpec((tq, hp * MLA_QK_PAD), qmap),
                  pl.BlockSpec((seq, hp * MLA_QK_PAD), kmap),
                  pl.BlockSpec((seq, hp * MLA_V), kmap)],
        out_specs=pl.BlockSpec((tq, hp * MLA_V), qmap),
        out_shape=jax.ShapeDtypeStruct((nb * seq, MLA_HEADS * MLA_V), BF16),
        compiler_params=_cparams(("parallel", "parallel", "arbitrary")),
        name="mla_prompt_attention",
    )(qc, kc, v)


def _absorb_q_kernel(q_ref, g_ref, w_ref, o_ref):
    o_ref[...] = _dot_nt(q_ref[...].astype(F32) * g_ref[...], w_ref[...]).astype(o_ref.dtype)


def _absorb_q(qc, g_k_nope, w_uk2, mp, ms):
    assert mp % ms == 0
    return pl.pallas_call(
        _absorb_q_kernel,
        grid=(MLA_HEADS,),
        in_specs=[pl.BlockSpec((ms, MLA_NOPE), lambda h: (mp // ms, h * (MLA_QK_PAD // MLA_NOPE))),
                  pl.BlockSpec((1, MLA_NOPE), lambda h: (0, 0)),
                  pl.BlockSpec((MLA_KV_RANK, MLA_NOPE), lambda h: (0, h))],
        out_specs=pl.BlockSpec((ms, MLA_KV_RANK), lambda h: (0, h)),
        out_shape=jax.ShapeDtypeStruct((ms, MLA_HEADS * MLA_KV_RANK), BF16),
        compiler_params=_cparams(("parallel",)),
        name="mla_absorb_q",
    )(qc, g_k_nope.reshape(1, -1), w_uk2)


def _expand_o_kernel(o_ref, w_ref, out_ref):
    out_ref[...] = _dot(o_ref[...], w_ref[...]).astype(out_ref.dtype)


def _expand_o(o_lat, w_uv2):
    ms = o_lat.shape[0]
    return pl.pallas_call(
        _expand_o_kernel,
        grid=(MLA_HEADS,),
        in_specs=[pl.BlockSpec((ms, MLA_KV_RANK), lambda h: (0, h)),
                  pl.BlockSpec((MLA_KV_RANK, MLA_V), lambda h: (0, h))],
        out_specs=pl.BlockSpec((ms, MLA_V), lambda h: (0, h)),
        out_shape=jax.ShapeDtypeStruct((ms, MLA_HEADS * MLA_V), BF16),
        compiler_params=_cparams(("parallel",)),
        name="mla_expand_o",
    )(o_lat, w_uv2)


def _softmax_update(sc, values, m_scr, l_scr, acc_scr):
    m_old = m_scr[...]
    m_new = jnp.maximum(m_old, jnp.max(sc, axis=-1, keepdims=True))
    corr = jnp.exp2(m_old - m_new)
    p = jnp.exp2(sc - m_new)
    l_scr[...] = l_scr[...] * corr + jnp.sum(p, axis=-1, keepdims=True)
    acc_scr[...] = acc_scr[...] * corr + _dot(p, values)
    m_scr[...] = m_new


def _sample_attn_kernel(pt_ref, ql_ref, qp_ref, *refs, pps, nsteps, steps, page):
    lat_refs = refs[:pps]
    pe_refs = refs[pps:2 * pps]
    ri_refs = refs[2 * pps:3 * pps]
    latn_ref, pen_ref, rin_ref, o_ref, m_scr, l_scr, acc_scr, lat_buf, pe_buf, ri_buf = refs[3 * pps:]
    s = pl.program_id(1)

    @pl.when(s == 0)
    def _():
        m_scr[...] = jnp.full_like(m_scr, -jnp.inf)
        l_scr[...] = jnp.zeros_like(l_scr)
        acc_scr[...] = jnp.zeros_like(acc_scr)

    for p in range(pps):
        keys = slice(p * page, (p + 1) * page)
        lat_buf[keys, :] = lat_refs[p][...].astype(BF16)
        pe_buf[:, keys] = pe_refs[p][...].astype(BF16)
        ri_buf[:, keys] = ri_refs[p][...]
    ql = ql_ref[...]
    qp = qp_ref[...]
    nsub = SAMPLE_SUBBLOCKS if pps % SAMPLE_SUBBLOCKS == 0 else 1
    width = pps * page // nsub
    subs = [slice(u * width, (u + 1) * width) for u in range(nsub)]
    lat = [lat_buf[sl, :] for sl in subs]
    sc = [(_dot_nt(ql, lat[u]) + _dot(qp, pe_buf[:, subs[u]])) * jnp.concatenate([ri_buf[:, subs[u]]] * steps, axis=0)
          for u in range(nsub)]
    for u in range(nsub):
        _softmax_update(sc[u], lat[u], m_scr, l_scr, acc_scr)

    @pl.when(s == nsteps - 1)
    def _():
        latn = latn_ref[...].astype(BF16)
        scn = (_dot_nt(ql, latn) + _dot(qp, pen_ref[...])) * jnp.concatenate([rin_ref[...]] * steps, axis=0)
        qstep = lax.broadcasted_iota(jnp.int32, scn.shape, 0) // MLA_HEADS
        kstep = lax.broadcasted_iota(jnp.int32, scn.shape, 1)
        scn = jnp.where(jnp.logical_and(kstep < steps, kstep <= qstep), scn, -jnp.inf)
        _softmax_update(scn, latn, m_scr, l_scr, acc_scr)
        o_ref[...] = (acc_scr[...] / l_scr[...]).astype(o_ref.dtype)


def _mla_sample_attention(q_lat, q_pe, lat_new, pe_new_t, rinv_new_t, cache_lat, cache_pe_t, cache_rinv_t, li,
                          page_table, steps):
    nb, n_pages = page_table.shape
    page = cache_lat.shape[2]
    pps = _tile(n_pages, PAGES_PER_STEP, 1)
    nsteps = n_pages // pps
    rows = q_lat.shape[1]
    npad = lat_new.shape[1]

    def page_spec(shape, p):
        return pl.BlockSpec((None, None) + shape, lambda b, s, pt: (li, pt[b * n_pages + s * pps + p], 0, 0))

    per_b = lambda r, w: pl.BlockSpec((None, r, w), lambda b, s, pt: (b, 0, 0))
    in_specs = ([per_b(rows, MLA_KV_RANK), per_b(rows, MLA_ROPE)]
                + [page_spec((page, MLA_KV_RANK), p) for p in range(pps)]
                + [page_spec((MLA_ROPE, page), p) for p in range(pps)]
                + [page_spec((MLA_HEADS, page), p) for p in range(pps)]
                + [per_b(npad, MLA_KV_RANK), per_b(MLA_ROPE, npad), per_b(MLA_HEADS, npad)])
    grid_spec = pltpu.PrefetchScalarGridSpec(
        num_scalar_prefetch=1,
        grid=(nb, nsteps),
        in_specs=in_specs,
        out_specs=per_b(rows, MLA_KV_RANK),
        scratch_shapes=[pltpu.VMEM((rows, 1), F32), pltpu.VMEM((rows, 1), F32),
                        pltpu.VMEM((rows, MLA_KV_RANK), F32),
                        pltpu.VMEM((pps * page, MLA_KV_RANK), BF16),
                        pltpu.VMEM((MLA_ROPE, pps * page), BF16),
                        pltpu.VMEM((MLA_HEADS, pps * page), F32)])
    return pl.pallas_call(
        functools.partial(_sample_attn_kernel, pps=pps, nsteps=nsteps, steps=steps, page=page),
        grid_spec=grid_spec,
        out_shape=jax.ShapeDtypeStruct((nb, rows, MLA_KV_RANK), BF16),
        compiler_params=_cparams(("parallel", "arbitrary")),
        name="mla_sample_attention",
    )(page_table.reshape(-1), q_lat, q_pe, *([cache_lat] * pps), *([cache_pe_t] * pps), *([cache_rinv_t] * pps),
      lat_new, pe_new_t.astype(BF16), rinv_new_t)


def _rope_tables(pos):
    half = MLA_ROPE // 2
    inv = ROPE_THETA ** (-jnp.arange(half, dtype=F32) / half)
    ang = pos.astype(F32)[:, None] * inv[None, :]
    cos, sin = jnp.cos(ang), jnp.sin(ang)
    return jnp.concatenate([cos, cos], axis=-1), jnp.concatenate([-sin, sin], axis=-1)


def _mla_mixer(h, x_res, cache_lat, cache_pe, cache_rinv, page_table, w_in, q_norm_g, kv_norm_g, w_uq, w_uk,
               w_uv, g_q, g_k, w_o, li, dims, past_len):
    mp, ms, nbp, seq, nbs, steps = dims
    pos = jnp.concatenate([jnp.tile(jnp.arange(seq), nbp), jnp.tile(past_len + jnp.arange(steps), nbs)])
    cos_t, sin_t = _rope_tables(pos)
    proj = _mm(h, w_in, (li,), name="mla_in_proj")
    w_uq3 = w_uq[li].reshape(MLA_Q_RANK, MLA_HEADS, MLA_QK)
    qc, qr = _mla_q(proj, q_norm_g[li], w_uq3[:, :, :MLA_NOPE].reshape(MLA_Q_RANK, -1),
                    w_uq3[:, :, MLA_NOPE:].reshape(MLA_Q_RANK, -1), g_q[li], cos_t, sin_t)
    w_uk2 = w_uk[li].reshape(MLA_KV_RANK, -1)
    w_uv2 = w_uv[li].reshape(MLA_KV_RANK, -1)
    lat, kpe, rinv, kc, v = _mla_kv(proj, proj[:, MLA_Q_RANK + MLA_KV_RANK:], kv_norm_g[li], w_uk2, w_uv2,
                                         g_k[li], cos_t, sin_t)
    o_p = _mla_prompt_attention(qc, kc, v, nbp, seq)

    q_lat = _absorb_q(qc, g_k[li, :MLA_NOPE], w_uk2, mp, ms)
    padn = lambda a: jnp.pad(a[mp:].reshape(nbs, steps, -1), ((0, 0), (0, 8 - steps), (0, 0)))
    o_lat = _mla_sample_attention(q_lat.reshape(nbs, steps * MLA_HEADS, MLA_KV_RANK),
                                  qr[mp:].reshape(nbs, steps * MLA_HEADS, MLA_ROPE),
                                  padn(lat), padn(kpe).swapaxes(1, 2), padn(rinv).swapaxes(1, 2), cache_lat,
                                  cache_pe.swapaxes(2, 3), cache_rinv.swapaxes(2, 3), li, page_table, steps)
    o_s = _expand_o(o_lat.reshape(ms, MLA_HEADS * MLA_KV_RANK), w_uv2)
    x_new = _mm(jnp.concatenate([o_p, o_s], axis=0), w_o, (li,), residual=x_res, tk=2048, name="mla_out_proj")
    return x_new, lat, kpe, rinv


def kernel(x_prompt, x_sample, state_gdn_ssm, state_gdn_conv, cache_mla_latent, cache_mla_k_pe, cache_mla_k_rinv,
           page_table, norm_g, ffn_w_gate, ffn_w_up, ffn_w_down, gdn_w_in, gdn_conv_w, gdn_a_log, gdn_dt_bias,
           gdn_norm_g, gdn_w_out, mla_w_in, mla_q_norm_g, mla_kv_norm_g, mla_w_uq, mla_w_uk, mla_w_uv,
           mla_qk_norm_q, mla_qk_norm_k, mla_w_o):
    nbp, seq, d = x_prompt.shape
    nbs, steps, _ = x_sample.shape
    mp, ms = nbp * seq, nbs * steps
    dims = (mp, ms, nbp, seq, nbs, steps)
    depth = norm_g.shape[0]
    past_len = page_table.shape[1] * cache_mla_latent.shape[2]
    x = jnp.concatenate([x_prompt.reshape(mp, d), x_sample.reshape(ms, d)], axis=0)
    w_down_bf16 = ffn_w_down.astype(BF16)
    h = _norm(x, norm_g[0, 0], BF16)
    outs = {k: [] for k in ("ssm_p", "conv_p", "ssm_s", "conv_s", "lat", "pe", "rinv")}
    for layer in range(depth):
        li = layer // 2
        x = _macaron_half(x, h, ffn_w_gate, ffn_w_up, w_down_bf16, layer, 0)
        h = _norm(x, norm_g[layer, 1], BF16)
        if layer % 2 == 0:
            x, ssm_p, conv_p, ssm_s, conv_s = _gdn_mixer(
                h, x, state_gdn_ssm, state_gdn_conv, gdn_w_in, gdn_conv_w, gdn_a_log, gdn_dt_bias, gdn_norm_g,
                gdn_w_out, li, dims)
            for key, val in (("ssm_p", ssm_p), ("conv_p", conv_p), ("ssm_s", ssm_s), ("conv_s", conv_s)):
                outs[key].append(val)
        else:
            x, lat, kpe, rinv = _mla_mixer(
                h, x, cache_mla_latent, cache_mla_k_pe, cache_mla_k_rinv, page_table, mla_w_in, mla_q_norm_g,
                mla_kv_norm_g, mla_w_uq, mla_w_uk, mla_w_uv, mla_qk_norm_q, mla_qk_norm_k, mla_w_o, li, dims,
                past_len)
            for key, val in (("lat", lat), ("pe", kpe), ("rinv", rinv)):
                outs[key].append(val)
        h = _norm(x, norm_g[layer, 2], BF16)
        x = _macaron_half(x, h, ffn_w_gate, ffn_w_up, w_down_bf16, layer, 1)
        if layer + 1 < depth:
            x, h = _norm2(x, norm_g[layer, 3], norm_g[layer + 1, 0])
        else:
            y_p = _norm(x, norm_g[layer, 3], F32, 0, mp).reshape(nbp, seq, d)
            y_s = _norm(x, norm_g[layer, 3], F32, mp, ms).reshape(nbs, steps, d)

    def split(a):
        return a[:mp].reshape(nbp, seq, -1), a[mp:].reshape(nbs, steps, -1)

    lat_p, lat_s = zip(*[split(a) for a in outs["lat"]])
    pe_p, pe_s = zip(*[split(a) for a in outs["pe"]])
    rinv_p, rinv_s = zip(*[split(a) for a in outs["rinv"]])
    return (y_p, y_s,
            jnp.stack(outs["ssm_p"]), jnp.stack(outs["conv_p"]), jnp.stack(lat_p), jnp.stack(pe_p),
            jnp.stack(rinv_p),
            jnp.stack(outs["ssm_s"]), jnp.stack(outs["conv_s"]), jnp.stack(lat_s), jnp.stack(pe_s),
            jnp.stack(rinv_s))
```

```python
import functools
import math

import jax
import jax.numpy as jnp
from jax import lax
from jax.experimental import pallas as pl
from jax.experimental.pallas import tpu as pltpu

F32 = jnp.float32
BF16 = jnp.bfloat16
EPS = 1e-6
ROPE_THETA = 10000.0

GDN_K_HEADS = 16
GDN_V_HEADS = 32
GDN_HEAD = 128
GDN_CONV = 4
GDN_CHUNK = 64
MLA_HEADS = 64
MLA_Q_RANK = 1024
MLA_KV_RANK = 512
MLA_NOPE = 128
MLA_ROPE = 64
MLA_V = 128
MLA_QK = MLA_NOPE + MLA_ROPE
MLA_QK_PAD = 256

LANE = 128
SUBLANE_BF16 = 16
VMEM_LIMIT = 56 * 1024 * 1024

FFN_DOWN_K_TILE = 3584
FFN_UP_ROW_TILE = 2176
ROW_TILE = 1088
GDN_BLOCK_ROWS = 256
GDN_GROUPS_PER_STEP = 8
MLA_Q_HEADS_PER_STEP = 8
MLA_KV_HEADS_PER_STEP = 8
FLASH_HEADS_PER_STEP = 8
FLASH_BLOCK = 512
PAGES_PER_STEP = 16
SAMPLE_SUBBLOCKS = 4
SAMPLE_PAD = 16


def _tile(n, target, align):
    if n <= target:
        return n
    t = (target // align) * align
    while t >= align:
        if n % t == 0:
            return t
        t -= align
    return n


def _cparams(sem):
    return pltpu.CompilerParams(dimension_semantics=sem, vmem_limit_bytes=VMEM_LIMIT)


def _dot(a, b):
    return jnp.dot(a.astype(BF16), b.astype(BF16), preferred_element_type=F32)


def _dot_nt(a, b):
    return lax.dot_general(a.astype(BF16), b.astype(BF16), (((1,), (1,)), ((), ())),
                           preferred_element_type=F32)


def _dot_tn(a, b):
    return lax.dot_general(a.astype(BF16), b.astype(BF16), (((0,), (0,)), ((), ())),
                           preferred_element_type=F32)


def _split3(x):
    x1 = x.astype(BF16)
    r1 = x - x1.astype(F32)
    x2 = r1.astype(BF16)
    x3 = (r1 - x2.astype(F32)).astype(BF16)
    return x1, x2, x3


def _rms(x, g):
    return x * lax.rsqrt(jnp.mean(x * x, axis=-1, keepdims=True) + EPS) * g


def _sigmoid(x):
    return 1.0 / (1.0 + jnp.exp(-x))


def _norm_kernel(x_ref, g_ref, o_ref):
    o_ref[...] = _rms(x_ref[...], g_ref[...]).astype(o_ref.dtype)


def _norm2_kernel(x_ref, g1_ref, g2_ref, y_ref, h_ref):
    y = _rms(x_ref[...], g1_ref[...])
    y_ref[...] = y
    h_ref[...] = _rms(y, g2_ref[...]).astype(h_ref.dtype)


def _norm(x, g, out_dtype, row0=0, nrows=None):
    d = x.shape[1]
    nrows = x.shape[0] if nrows is None else nrows
    tm = _tile(math.gcd(row0, nrows), 512, SUBLANE_BF16)
    return pl.pallas_call(
        _norm_kernel,
        grid=(nrows // tm,),
        in_specs=[pl.BlockSpec((tm, d), lambda i: (i + row0 // tm, 0)), pl.BlockSpec((1, d), lambda i: (0, 0))],
        out_specs=pl.BlockSpec((tm, d), lambda i: (i, 0)),
        out_shape=jax.ShapeDtypeStruct((nrows, d), out_dtype),
        compiler_params=_cparams(("parallel",)),
        name="rmsnorm",
    )(x, g.reshape(1, d))


def _norm2(x, g1, g2):
    m, d = x.shape
    tm = _tile(m, 512, SUBLANE_BF16)
    return pl.pallas_call(
        _norm2_kernel,
        grid=(m // tm,),
        in_specs=[pl.BlockSpec((tm, d), lambda i: (i, 0)), pl.BlockSpec((1, d), lambda i: (0, 0)),
                  pl.BlockSpec((1, d), lambda i: (0, 0))],
        out_specs=[pl.BlockSpec((tm, d), lambda i: (i, 0)), pl.BlockSpec((tm, d), lambda i: (i, 0))],
        out_shape=[jax.ShapeDtypeStruct((m, d), F32), jax.ShapeDtypeStruct((m, d), BF16)],
        compiler_params=_cparams(("parallel",)),
        name="rmsnorm_pair",
    )(x, g1.reshape(1, d), g2.reshape(1, d))


def _mm_kernel(*refs, has_res, nk, w_is_nk):
    if has_res:
        a_ref, w_ref, r_ref, o_ref = refs
    else:
        a_ref, w_ref, o_ref = refs
    part = (_dot_nt if w_is_nk else _dot)(a_ref[...], w_ref[...])
    if nk == 1:
        if has_res:
            part = part + r_ref[...]
        o_ref[...] = part.astype(o_ref.dtype)
        return
    k = pl.program_id(2)

    @pl.when(k == 0)
    def _():
        o_ref[...] = part + r_ref[...] if has_res else part

    @pl.when(k > 0)
    def _():
        o_ref[...] += part


def _mm(a, w, wprefix=(), *, n_out=None, out_dtype=F32, residual=None, tm=ROW_TILE, tn=1024, tk=1024,
        w_is_nk=False, name="matmul"):
    m, kdim = a.shape
    assert w.shape[-1 if w_is_nk else -2] == kdim
    n = n_out or w.shape[-2 if w_is_nk else -1]
    tm = _tile(m, tm, SUBLANE_BF16)
    tn = _tile(n, tn, LANE)
    tk = _tile(kdim, tk, LANE)
    nk = kdim // tk
    assert nk == 1 or out_dtype == F32
    npre = len(wprefix)
    in_specs = [pl.BlockSpec((tm, tk), lambda i, j, k: (i, k)),
                pl.BlockSpec((None,) * npre + (tn, tk), lambda i, j, k: tuple(wprefix) + (j, k)) if w_is_nk else
                pl.BlockSpec((None,) * npre + (tk, tn), lambda i, j, k: tuple(wprefix) + (k, j))]
    args = [a, w]
    if residual is not None:
        in_specs.append(pl.BlockSpec((tm, tn), lambda i, j, k: (i, j)))
        args.append(residual)
    return pl.pallas_call(
        functools.partial(_mm_kernel, has_res=residual is not None, nk=nk, w_is_nk=w_is_nk),
        grid=(m // tm, n // tn, nk),
        in_specs=in_specs,
        out_specs=pl.BlockSpec((tm, tn), lambda i, j, k: (i, j)),
        out_shape=jax.ShapeDtypeStruct((m, n), out_dtype),
        compiler_params=_cparams(("parallel", "parallel", "arbitrary")),
        name=name,
    )(*args)


def _ffn_up_kernel(h_ref, wg_ref, wu_ref, o_ref):
    h = h_ref[...]
    gate = _dot(h, wg_ref[...])
    up = _dot(h, wu_ref[...])
    o_ref[...] = (0.5 * gate * _sigmoid(gate) * up).astype(o_ref.dtype)


def _ffn_up(h, w_gate, w_up, layer, half):
    m, d = h.shape
    f_dim = w_gate.shape[-1]
    tm = _tile(m, FFN_UP_ROW_TILE, SUBLANE_BF16)
    tf = _tile(f_dim, 256, LANE)
    wspec = pl.BlockSpec((None, None, d, tf), lambda i, f: (layer, half, 0, f))
    return pl.pallas_call(
        _ffn_up_kernel,
        grid=(m // tm, f_dim // tf),
        in_specs=[pl.BlockSpec((tm, d), lambda i, f: (i, 0), pipeline_mode=pl.Buffered(1)), wspec, wspec],
        out_specs=pl.BlockSpec((tm, tf), lambda i, f: (i, f)),
        out_shape=jax.ShapeDtypeStruct((m, f_dim), BF16),
        compiler_params=_cparams(("parallel", "arbitrary")),
        name="ffn_up",
    )(h, w_gate, w_up)


def _split_contraction(kdim, target):
    tk = _tile(kdim, target, LANE)
    if kdim // tk <= 8:
        return tk, kdim // tk, 0
    for tail in (2 * LANE, LANE):
        main = kdim - tail
        tk = _tile(main, target, LANE)
        if main % tail == 0 and main // tk <= 8:
            return tk, main // tk, tail
    raise ValueError(f"no contraction tiling for {kdim}")


def _ffn_down_kernel(a_ref, w_ref, at_ref, wt_ref, r_ref, o_ref):
    k = pl.program_id(2)
    part = _dot(a_ref[...], w_ref[...])

    @pl.when(k == 0)
    def _():
        o_ref[...] = r_ref[...] + part + _dot(at_ref[...], wt_ref[...])

    @pl.when(k > 0)
    def _():
        o_ref[...] += part


def _macaron_half(x, h, w_gate, w_up, w_down_bf16, layer, half):
    act = _ffn_up(h, w_gate, w_up, layer, half)
    m, f_dim = act.shape
    n = w_down_bf16.shape[-1]
    tk, nk, tail = _split_contraction(f_dim, FFN_DOWN_K_TILE)
    if tail == 0:
        return _mm(act, w_down_bf16, (layer, half), residual=x, tk=tk, name="ffn_down")
    tm = _tile(m, ROW_TILE, SUBLANE_BF16)
    tn = _tile(n, 1024, LANE)
    tail_blk = nk * tk // tail
    return pl.pallas_call(
        _ffn_down_kernel,
        grid=(m // tm, n // tn, nk),
        in_specs=[pl.BlockSpec((tm, tk), lambda i, j, k: (i, k)),
                  pl.BlockSpec((None, None, tk, tn), lambda i, j, k: (layer, half, k, j)),
                  pl.BlockSpec((tm, tail), lambda i, j, k: (i, tail_blk)),
                  pl.BlockSpec((None, None, tail, tn), lambda i, j, k: (layer, half, tail_blk, j)),
                  pl.BlockSpec((tm, tn), lambda i, j, k: (i, j))],
        out_specs=pl.BlockSpec((tm, tn), lambda i, j, k: (i, j)),
        out_shape=jax.ShapeDtypeStruct((m, n), F32),
        compiler_params=_cparams(("parallel", "parallel", "arbitrary")),
        name="ffn_down",
    )(act, w_down_bf16, act, w_down_bf16, x)


def _gates_kernel(b_ref, a_ref, alog_ref, dt_ref, beta_ref, g_ref):
    beta_ref[...] = _sigmoid(b_ref[...])
    x = a_ref[...] + dt_ref[...]
    softplus = jnp.maximum(x, 0.0) + jnp.log1p(jnp.exp(-jnp.abs(x)))
    g_ref[...] = -jnp.exp(alog_ref[...]) * softplus


def _gdn_gates(beta_in, a_in, a_log, dt_bias):
    m, h = beta_in.shape
    tm = _tile(m, 2176, 8)
    row = pl.BlockSpec((tm, h), lambda i: (i, 0))
    vec = pl.BlockSpec((1, h), lambda i: (0, 0))
    return pl.pallas_call(
        _gates_kernel,
        grid=(m // tm,),
        in_specs=[row, row, vec, vec],
        out_specs=[row, row],
        out_shape=[jax.ShapeDtypeStruct((m, h), F32)] * 2,
        compiler_params=_cparams(("parallel",)),
        name="gdn_gates",
    )(beta_in, a_in, a_log.reshape(1, h), dt_bias.reshape(1, h))


def _conv_finish(y, o_ref, row_sl, is_qk, qk_scale):
    y = y * _sigmoid(y)
    tc = y.shape[-1]

    @pl.when(is_qk)
    def _():
        for h in range(tc // GDN_HEAD):
            sl = slice(h * GDN_HEAD, (h + 1) * GDN_HEAD)
            yh = y[:, sl]
            r = lax.rsqrt(jnp.sum(yh * yh, axis=-1, keepdims=True) + EPS)
            o_ref[row_sl + (sl,)] = (yh * (r * qk_scale)).astype(o_ref.dtype)

    @pl.when(jnp.logical_not(is_qk))
    def _():
        o_ref[row_sl + (slice(None),)] = y.astype(o_ref.dtype)


def _conv_prompt_kernel(x_ref, halo_ref, w_ref, o_ref, *, tiles_per_seq, nq_tiles, nqk_tiles):
    i = pl.program_id(0)
    c = pl.program_id(1)
    x = x_ref[...]
    tt = x.shape[0]
    halo = jnp.where(i % tiles_per_seq == 0, 0.0, halo_ref[...])
    xp = jnp.concatenate([halo, x], axis=0)
    w = w_ref[...]
    hl = halo.shape[0]
    y = pltpu.roll(xp, GDN_CONV - 1, 0)[hl:] * w[0:1]
    for j in range(1, GDN_CONV - 1):
        y = y + pltpu.roll(xp, GDN_CONV - 1 - j, 0)[hl:] * w[j:j + 1]
    y = y + x * w[GDN_CONV - 1:GDN_CONV]
    qk_scale = jnp.where(c < nq_tiles, GDN_HEAD ** -0.5, 1.0)
    _conv_finish(y, o_ref, (slice(None),), c < nqk_tiles, qk_scale)


def _conv_sample_kernel(x_ref, w_ref, o_ref, *, steps, nq_tiles, nqk_tiles):
    c = pl.program_id(0)
    w = w_ref[...]
    qk_scale = jnp.where(c < nq_tiles, GDN_HEAD ** -0.5, 1.0)
    for s in range(steps):
        y = x_ref[s] * w[0:1]
        for j in range(1, GDN_CONV):
            y = y + x_ref[s + j] * w[j:j + 1]
        _conv_finish(y, o_ref, (s, slice(None)), c < nqk_tiles, qk_scale)


def _gdn_conv_prompt(qkvz, conv_w, mp, seq):
    cdim = conv_w.shape[-1]
    tt = _tile(seq, 256, SUBLANE_BF16)
    tc = _tile(2 * GDN_K_HEADS * GDN_HEAD // 2, 512, GDN_HEAD)
    nq = GDN_K_HEADS * GDN_HEAD // tc
    return pl.pallas_call(
        functools.partial(_conv_prompt_kernel, tiles_per_seq=seq // tt, nq_tiles=nq, nqk_tiles=2 * nq),
        grid=(mp // tt, cdim // tc),
        in_specs=[pl.BlockSpec((tt, tc), lambda i, c: (i, c)),
                  pl.BlockSpec((SUBLANE_BF16, tc), lambda i, c: (jnp.maximum(i * (tt // SUBLANE_BF16) - 1, 0), c)),
                  pl.BlockSpec((GDN_CONV, tc), lambda i, c: (0, c))],
        out_specs=pl.BlockSpec((tt, tc), lambda i, c: (i, c)),
        out_shape=jax.ShapeDtypeStruct((mp, cdim), BF16),
        compiler_params=_cparams(("parallel", "parallel")),
        name="gdn_conv_prompt",
    )(qkvz, qkvz, conv_w)


def _gdn_conv_sample(xpad_t, conv_w, steps):
    tp, nb, cdim = xpad_t.shape
    tc = _tile(GDN_K_HEADS * GDN_HEAD, 512, GDN_HEAD)
    nq = GDN_K_HEADS * GDN_HEAD // tc
    return pl.pallas_call(
        functools.partial(_conv_sample_kernel, steps=steps, nq_tiles=nq, nqk_tiles=2 * nq),
        grid=(cdim // tc,),
        in_specs=[pl.BlockSpec((tp, nb, tc), lambda c: (0, 0, c)),
                  pl.BlockSpec((GDN_CONV, tc), lambda c: (0, c))],
        out_specs=pl.BlockSpec((steps, nb, tc), lambda c: (0, 0, c)),
        out_shape=jax.ShapeDtypeStruct((steps, nb, cdim), BF16),
        compiler_params=_cparams(("parallel",)),
        name="gdn_conv_sample",
    )(xpad_t, conv_w)


def _gdn_chunk_kernel(*refs, c, hb, ngroups, rep, nchunks, has_s0):
    if has_s0:
        q_ref, k_ref, v_ref, z_ref, g_ref, b_ref, ng_ref, s0_ref, o_ref, sfin_ref, s_scr = refs
    else:
        q_ref, k_ref, v_ref, z_ref, g_ref, b_ref, ng_ref, o_ref, sfin_ref, s_scr = refs
    n = pl.program_id(2)

    @pl.when(n == 0)
    def _():
        s_scr[...] = s0_ref[...] if has_s0 else jnp.zeros_like(s_scr)

    d = GDN_HEAD
    r = hb * c
    rowi = lax.broadcasted_iota(jnp.int32, (r, r), 0)
    coli = lax.broadcasted_iota(jnp.int32, (r, r), 1)
    same = (rowi ^ coli) < c
    tril = jnp.logical_and(same, rowi >= coli)
    strict = jnp.logical_and(same, rowi > coli)
    triu = jnp.logical_and(same, rowi <= coli)
    eye = rowi == coli
    eye_f = eye.astype(F32)
    head_of_row = lax.broadcasted_iota(jnp.int32, (r, hb), 0) // c == lax.broadcasted_iota(jnp.int32, (r, hb), 1)
    ng = ng_ref[...]

    def stack(ref, heads):
        return jnp.concatenate([ref[:, h * d:(h + 1) * d] for h in heads], axis=0)

    def stacked_column(ref, j0):
        tiled = jnp.concatenate([ref[:, j0:j0 + hb]] * hb, axis=0)
        return jnp.sum(jnp.where(head_of_row, tiled, 0.0), axis=1, keepdims=True)

    groups = range(ngroups)
    vheads = [list(range(gi * hb, (gi + 1) * hb)) for gi in groups]
    qst = [stack(q_ref, [h // rep for h in vheads[gi]]) for gi in groups]
    kst = [stack(k_ref, [h // rep for h in vheads[gi]]) for gi in groups]
    kf = [kst[gi].astype(F32) for gi in groups]
    vf = [stack(v_ref, vheads[gi]).astype(F32) for gi in groups]
    g = [stacked_column(g_ref, gi * hb) for gi in groups]
    beta = [stacked_column(b_ref, gi * hb) for gi in groups]
    qk_kk = [_dot_nt(jnp.concatenate([qst[gi], kst[gi]], axis=0), kst[gi]) for gi in groups]
    g_row = [jnp.sum(jnp.where(eye, g[gi], 0.0), axis=0, keepdims=True) for gi in groups]
    gc_col = [jnp.sum(jnp.where(tril, g_row[gi], 0.0), axis=1, keepdims=True) for gi in groups]
    gc_row = [jnp.sum(jnp.where(triu, g[gi], 0.0), axis=0, keepdims=True) for gi in groups]
    decay = [jnp.where(tril, jnp.exp(jnp.where(tril, gc_col[gi] - gc_row[gi], 0.0)), 0.0) for gi in groups]
    qk = [qk_kk[gi][:r] * decay[gi] for gi in groups]
    mm = [-jnp.where(strict, qk_kk[gi][r:] * beta[gi] * decay[gi], 0.0) for gi in groups]
    inv = [eye_f + mm[gi] for gi in groups]
    for _ in range(int(math.log2(c)) - 1):
        mm = [_dot(mm[gi], mm[gi]) for gi in groups]
        inv = [inv[gi] + _dot(inv[gi], mm[gi]) for gi in groups]
    e_col = [jnp.exp(gc_col[gi]) for gi in groups]
    uw = [_dot(inv[gi], jnp.concatenate([vf[gi] * beta[gi], kf[gi] * (beta[gi] * e_col[gi])], axis=1))
          for gi in groups]
    qd = [qst[gi].astype(F32) * e_col[gi] for gi in groups]
    heads = [(gi, j) for gi in groups for j in range(hb)]
    rows = [slice(j * c, (j + 1) * c) for j in range(hb)]
    s_old = {(gi, j): s_scr[gi * hb + j] for gi, j in heads}
    wq = {(gi, j): _dot(jnp.concatenate([uw[gi][rows[j], d:], qd[gi][rows[j]]], axis=0), s_old[gi, j])
          for gi, j in heads}
    v_new = {(gi, j): uw[gi][rows[j], :d] - wq[gi, j][:c] for gi, j in heads}
    for gi, j in heads:
        gc_last = gc_col[gi][(j + 1) * c - 1:(j + 1) * c, :]
        k_dec = kf[gi][rows[j]] * jnp.exp(gc_last - gc_col[gi][rows[j]])
        s_scr[gi * hb + j] = s_old[gi, j] * jnp.exp(gc_last) + _dot_tn(k_dec, v_new[gi, j])
    o = [jnp.concatenate([wq[gi, j][c:] for j in range(hb)], axis=0)
         + _dot(qk[gi], jnp.concatenate([v_new[gi, j] for j in range(hb)], axis=0)) for gi in groups]
    for gi in groups:
        z = stack(z_ref, vheads[gi])
        out = (_rms(o[gi], ng) * (z * _sigmoid(z))).astype(o_ref.dtype)
        for j in range(hb):
            o_ref[:, (gi * hb + j) * d:(gi * hb + j + 1) * d] = out[rows[j]]

    @pl.when(n == nchunks - 1)
    def _():
        sfin_ref[...] = s_scr[...]


def _gdn_rule(qkv, z, z_col0, g, beta, norm_g, s0, nb, seq, c):
    d = GDN_HEAD
    hb = min(GDN_BLOCK_ROWS // c, GDN_V_HEADS)
    ngroups = min(GDN_GROUPS_PER_STEP, GDN_V_HEADS // hb)
    hs = hb * ngroups
    rep = GDN_V_HEADS // GDN_K_HEADS
    nchunks = seq // c
    koff = GDN_K_HEADS // (hs // rep)
    voff = 2 * GDN_K_HEADS // hs
    zoff = z_col0 // (hs * d)
    rowmap = lambda b, h, n: (b * nchunks + n, h)
    gatemap = lambda b, h, n: (h, b * nchunks + n, 0)
    in_specs = [pl.BlockSpec((c, hs // rep * d), rowmap),
                pl.BlockSpec((c, hs // rep * d), lambda b, h, n: (b * nchunks + n, koff + h)),
                pl.BlockSpec((c, hs * d), lambda b, h, n: (b * nchunks + n, voff + h)),
                pl.BlockSpec((c, hs * d), lambda b, h, n: (b * nchunks + n, zoff + h)),
                pl.BlockSpec((None, c, hs), gatemap),
                pl.BlockSpec((None, c, hs), gatemap),
                pl.BlockSpec((1, d), lambda b, h, n: (0, 0))]
    args = [qkv, qkv, qkv, z, _head_groups(g, hs), _head_groups(beta, hs), norm_g.reshape(1, d)]
    if s0 is not None:
        in_specs.append(pl.BlockSpec((None, hs, d, d), lambda b, h, n: (b, h, 0, 0)))
        args.append(s0)
    return pl.pallas_call(
        functools.partial(_gdn_chunk_kernel, c=c, hb=hb, ngroups=ngroups, rep=rep, nchunks=nchunks,
                          has_s0=s0 is not None),
        grid=(nb, GDN_V_HEADS // hs, nchunks),
        in_specs=in_specs,
        out_specs=[pl.BlockSpec((c, hs * d), rowmap),
                   pl.BlockSpec((None, hs, d, d), lambda b, h, n: (b, h, 0, 0))],
        out_shape=[jax.ShapeDtypeStruct((nb * seq, GDN_V_HEADS * d), BF16),
                   jax.ShapeDtypeStruct((nb, GDN_V_HEADS, d, d), F32)],
        scratch_shapes=[pltpu.VMEM((hs, d, d), F32)],
        compiler_params=_cparams(("parallel", "parallel", "arbitrary")),
        name="gdn_rule",
    )(*args)


def _head_groups(x, hg):
    r, h = x.shape
    return x.reshape(r, h // hg, hg).transpose(1, 0, 2)


def _pad_steps(x, nb, steps, pad_to):
    cdim = x.shape[-1]
    x = x.reshape(nb, steps, cdim)
    return jnp.pad(x, ((0, 0), (0, pad_to - steps), (0, 0))).reshape(nb * pad_to, cdim)


def _gdn_mixer(h, x_res, state_ssm, state_conv, w_in, conv_w, a_log, dt_bias, norm_g, w_out, li, dims):
    mp, ms, nbp, seq, nbs, steps = dims
    cdim = conv_w.shape[-1]
    vdim = GDN_V_HEADS * GDN_HEAD
    w_in_t = jnp.swapaxes(w_in, 1, 2)
    qkvz = _mm(h, w_in_t, (li,), n_out=cdim + vdim, tn=512, tk=h.shape[1], w_is_nk=True, name="gdn_in_proj")
    ba = _mm(h, w_in_t[li, cdim + vdim:], w_is_nk=True, name="gdn_gate_proj")
    beta, g = _gdn_gates(ba[:, :GDN_V_HEADS], ba[:, GDN_V_HEADS:], a_log[li], dt_bias[li])

    qkv_p = _gdn_conv_prompt(qkvz, conv_w[li], mp, seq)
    o_p, ssm_p = _gdn_rule(qkv_p, qkvz, cdim, g[:mp], beta[:mp], norm_g[li], None, nbp, seq,
                           GDN_CHUNK if seq % GDN_CHUNK == 0 else seq)
    conv_p = jnp.stack([qkvz[(b + 1) * seq - (GDN_CONV - 1):(b + 1) * seq, :cdim] for b in range(nbp)])

    qkv_s = qkvz[mp:, :cdim].reshape(nbs, steps, cdim)
    win = jnp.concatenate([state_conv[li], qkv_s], axis=1)
    conv_s = win[:, win.shape[1] - (GDN_CONV - 1):]
    win_t = jnp.pad(win, ((0, 0), (0, 8 - win.shape[1] % 8 if win.shape[1] % 8 else 0), (0, 0))).transpose(1, 0, 2)
    qkv_c = _gdn_conv_sample(win_t, conv_w[li], steps).transpose(1, 0, 2).reshape(ms, cdim)
    pad = lambda a: _pad_steps(a, nbs, steps, SAMPLE_PAD)
    o_s, ssm_s = _gdn_rule(pad(qkv_c), pad(qkvz[mp:, cdim:cdim + vdim]), 0, pad(g[mp:]), pad(beta[mp:]),
                           norm_g[li], state_ssm[li], nbs, SAMPLE_PAD, SAMPLE_PAD)
    o_s = o_s.reshape(nbs, SAMPLE_PAD, vdim)[:, :steps].reshape(ms, vdim)
    o_all = jnp.concatenate([o_p, o_s], axis=0)
    x_new = _mm(o_all, w_out, (li,), residual=x_res, tk=2048, name="gdn_out_proj")
    return x_new, ssm_p, conv_p, ssm_s, conv_s


def _mla_q_kernel(cq_ref, gq_ref, wn_ref, wr_ref, gn_ref, gr_ref, cos_ref, sin_ref, en_ref, er_ref, qc_ref, qr_ref,
                  cqn_scr, *, hg, scale):
    j = pl.program_id(1)

    @pl.when(j == 0)
    def _():
        cqn_scr[...] = _rms(cq_ref[...], gq_ref[...]).astype(BF16)

    a = cqn_scr[...]
    qn = _dot(a, wn_ref[...])
    qr = _dot(a, wr_ref[...])
    width = hg * MLA_ROPE
    half = MLA_ROPE // 2
    xg = qr * gr_ref[...]
    lane = lax.broadcasted_iota(jnp.int32, xg.shape, 1)
    swapped = jnp.where(lane % MLA_ROPE < half, pltpu.roll(xg, width - half, 1), pltpu.roll(xg, half, 1))
    roped = xg * cos_ref[...] + swapped * sin_ref[...]
    gn = gn_ref[...]
    ss = _dot(qn * qn, en_ref[...]) + _dot(qr * qr, er_ref[...])
    r_all = lax.rsqrt(ss / MLA_QK + EPS) * scale
    for h in range(hg):
        ns = slice(h * MLA_NOPE, (h + 1) * MLA_NOPE)
        rs = slice(h * MLA_ROPE, (h + 1) * MLA_ROPE)
        an = qn[:, ns]
        r = r_all[:, h:h + 1]
        q_rope = (roped[:, rs] * r).astype(qr_ref.dtype)
        base = h * MLA_QK_PAD
        qc_ref[:, base:base + MLA_NOPE] = (an * r * gn).astype(qc_ref.dtype)
        qc_ref[:, base + MLA_NOPE:base + MLA_QK] = q_rope
        qc_ref[:, base + MLA_QK:base + MLA_QK_PAD] = jnp.zeros((an.shape[0], MLA_QK_PAD - MLA_QK), qc_ref.dtype)
        qr_ref[:, rs] = q_rope


def _mla_q(proj, q_norm_g, w_uq_nope, w_uq_rope, g_q, cos_t, sin_t):
    m = proj.shape[0]
    hg = MLA_Q_HEADS_PER_STEP
    tm = _tile(m, ROW_TILE, SUBLANE_BF16)
    gr_t = jnp.tile(g_q[MLA_NOPE:], hg).reshape(1, hg * MLA_ROPE)
    head_of = lambda width: (jnp.arange(hg * width)[:, None] // width == jnp.arange(hg)[None, :]).astype(BF16)
    return pl.pallas_call(
        functools.partial(_mla_q_kernel, hg=hg, scale=MLA_QK ** -0.5 * math.log2(math.e)),
        grid=(m // tm, MLA_HEADS // hg),
        in_specs=[pl.BlockSpec((tm, MLA_Q_RANK), lambda i, j: (i, 0)),
                  pl.BlockSpec((1, MLA_Q_RANK), lambda i, j: (0, 0)),
                  pl.BlockSpec((MLA_Q_RANK, hg * MLA_NOPE), lambda i, j: (0, j)),
                  pl.BlockSpec((MLA_Q_RANK, hg * MLA_ROPE), lambda i, j: (0, j)),
                  pl.BlockSpec((1, MLA_NOPE), lambda i, j: (0, 0)),
                  pl.BlockSpec((1, hg * MLA_ROPE), lambda i, j: (0, 0)),
                  pl.BlockSpec((tm, hg * MLA_ROPE), lambda i, j: (i, 0)),
                  pl.BlockSpec((tm, hg * MLA_ROPE), lambda i, j: (i, 0)),
                  pl.BlockSpec((hg * MLA_NOPE, hg), lambda i, j: (0, 0)),
                  pl.BlockSpec((hg * MLA_ROPE, hg), lambda i, j: (0, 0))],
        out_specs=[pl.BlockSpec((tm, hg * MLA_QK_PAD), lambda i, j: (i, j)),
                   pl.BlockSpec((tm, hg * MLA_ROPE), lambda i, j: (i, j))],
        out_shape=[jax.ShapeDtypeStruct((m, MLA_HEADS * MLA_QK_PAD), BF16),
                   jax.ShapeDtypeStruct((m, MLA_HEADS * MLA_ROPE), BF16)],
        scratch_shapes=[pltpu.VMEM((tm, MLA_Q_RANK), BF16)],
        compiler_params=_cparams(("parallel", "arbitrary")),
        name="mla_q",
    )(proj, q_norm_g.reshape(1, -1), w_uq_nope, w_uq_rope, g_q[:MLA_NOPE].reshape(1, -1), gr_t,
      jnp.tile(cos_t, (1, hg)), jnp.tile(sin_t, (1, hg)), head_of(MLA_NOPE), head_of(MLA_ROPE))


def _mla_kv_kernel(ckv_ref, kr_ref, gkv_ref, wuk_ref, wuv_ref, gkn_ref, gkr_ref, cos_ref, sin_ref, swap_ref,
                   lat_ref, kpe_ref, rinv_ref, kc_ref, v_ref, lat_scr, kpe_scr, *, hg):
    j = pl.program_id(1)

    @pl.when(j == 0)
    def _():
        lat = _rms(ckv_ref[...], gkv_ref[...])
        lat_ref[...] = lat
        lat_scr[...] = lat.astype(BF16)
        xg = kr_ref[...] * gkr_ref[...]
        x1, x2, x3 = _split3(xg)
        sw = swap_ref[...]
        swapped = _dot(x1, sw) + _dot(x2, sw) + _dot(x3, sw)
        kpe = xg * cos_ref[...] + swapped * sin_ref[...]
        kpe_ref[...] = kpe
        kpe_scr[...] = kpe
        rinv_ref[...] = jnp.zeros_like(rinv_ref)

    a = lat_scr[...]
    kn = _dot(a, wuk_ref[...])
    v_ref[...] = _dot(a, wuv_ref[...]).astype(v_ref.dtype)
    kr = kr_ref[...]
    ssr = jnp.sum(kr * kr, axis=-1, keepdims=True)
    kpe = kpe_scr[...]
    lane = lax.broadcasted_iota(jnp.int32, rinv_ref.shape, 1)
    racc = rinv_ref[...]
    gkn = gkn_ref[...]
    for h in range(hg):
        ns = slice(h * MLA_NOPE, (h + 1) * MLA_NOPE)
        x = kn[:, ns]
        ri = lax.rsqrt((jnp.sum(x * x, axis=-1, keepdims=True) + ssr) / MLA_QK + EPS)
        base = h * MLA_QK_PAD
        kc_ref[:, base:base + MLA_NOPE] = (x * gkn * ri).astype(kc_ref.dtype)
        kc_ref[:, base + MLA_NOPE:base + MLA_QK] = (kpe * ri).astype(kc_ref.dtype)
        kc_ref[:, base + MLA_QK:base + MLA_QK_PAD] = jnp.zeros((x.shape[0], MLA_QK_PAD - MLA_QK), kc_ref.dtype)
        racc = jnp.where(lane == j * hg + h, ri, racc)
    rinv_ref[...] = racc


def _mla_kv(proj, k_r, kv_norm_g, w_uk2, w_uv2, g_k, cos_t, sin_t):
    m = proj.shape[0]
    hg = MLA_KV_HEADS_PER_STEP
    tm = _tile(m, ROW_TILE, SUBLANE_BF16)
    half = MLA_ROPE // 2
    idx = jnp.arange(MLA_ROPE)
    swap = (idx[:, None] == (idx[None, :] + half) % MLA_ROPE).astype(BF16)
    row = lambda w: pl.BlockSpec((tm, w), lambda i, j: (i, 0))
    vec = lambda w: pl.BlockSpec((1, w), lambda i, j: (0, 0))
    return pl.pallas_call(
        functools.partial(_mla_kv_kernel, hg=hg),
        grid=(m // tm, MLA_HEADS // hg),
        in_specs=[pl.BlockSpec((tm, MLA_KV_RANK), lambda i, j: (i, MLA_Q_RANK // MLA_KV_RANK)),
                  row(MLA_ROPE), vec(MLA_KV_RANK),
                  pl.BlockSpec((MLA_KV_RANK, hg * MLA_NOPE), lambda i, j: (0, j)),
                  pl.BlockSpec((MLA_KV_RANK, hg * MLA_V), lambda i, j: (0, j)),
                  vec(MLA_NOPE), vec(MLA_ROPE), row(MLA_ROPE), row(MLA_ROPE),
                  pl.BlockSpec((MLA_ROPE, MLA_ROPE), lambda i, j: (0, 0))],
        out_specs=[row(MLA_KV_RANK), row(MLA_ROPE), row(MLA_HEADS),
                   pl.BlockSpec((tm, hg * MLA_QK_PAD), lambda i, j: (i, j)),
                   pl.BlockSpec((tm, hg * MLA_V), lambda i, j: (i, j))],
        out_shape=[jax.ShapeDtypeStruct((m, MLA_KV_RANK), F32),
                   jax.ShapeDtypeStruct((m, MLA_ROPE), F32),
                   jax.ShapeDtypeStruct((m, MLA_HEADS), F32),
                   jax.ShapeDtypeStruct((m, MLA_HEADS * MLA_QK_PAD), BF16),
                   jax.ShapeDtypeStruct((m, MLA_HEADS * MLA_V), BF16)],
        scratch_shapes=[pltpu.VMEM((tm, MLA_KV_RANK), BF16), pltpu.VMEM((tm, MLA_ROPE), F32)],
        compiler_params=_cparams(("parallel", "arbitrary")),
        name="mla_kv",
    )(proj, k_r, kv_norm_g.reshape(1, -1), w_uk2, w_uv2, g_k[:MLA_NOPE].reshape(1, -1),
      g_k[MLA_NOPE:].reshape(1, -1), cos_t, sin_t, swap)


def _flash_kernel(q_ref, k_ref, v_ref, o_ref, *, tq, hp):
    i = pl.program_id(2)
    krow = lax.broadcasted_iota(jnp.int32, (tq, tq), 0)
    qcol = lax.broadcasted_iota(jnp.int32, (tq, tq), 1)
    heads = range(hp)
    csl = [slice(h * MLA_QK_PAD, (h + 1) * MLA_QK_PAD) for h in heads]
    vsl = [slice(h * MLA_V, (h + 1) * MLA_V) for h in heads]
    q = [q_ref[:, csl[h]] for h in heads]

    def block(jb, carry, on_diagonal):
        ks = pl.ds(pl.multiple_of(jb * tq, tq), tq)
        s = [_dot_nt(k_ref[ks, csl[h]], q[h]) for h in heads]
        if on_diagonal:
            s = [jnp.where(krow <= qcol, s[h], -jnp.inf) for h in heads]
        m_new = [jnp.maximum(carry[h][0], jnp.max(s[h], axis=0, keepdims=True)) for h in heads]
        corr = [jnp.exp2(carry[h][0] - m_new[h]) for h in heads]
        p = [jnp.exp2(s[h] - m_new[h]) for h in heads]
        l = [carry[h][1] * corr[h] + jnp.sum(p[h], axis=0, keepdims=True) for h in heads]
        acc = [carry[h][2] * corr[h] + _dot_tn(v_ref[ks, vsl[h]], p[h]) for h in heads]
        return tuple((m_new[h], l[h], acc[h]) for h in heads)

    init = tuple((jnp.full((1, tq), -jnp.inf, F32), jnp.zeros((1, tq), F32), jnp.zeros((MLA_V, tq), F32))
                 for _ in heads)
    carry = lax.fori_loop(0, i, lambda jb, c: block(jb, c, False), init)
    carry = block(i, carry, True)
    ri = lax.broadcasted_iota(jnp.int32, (MLA_V, MLA_V), 0)
    ci = lax.broadcasted_iota(jnp.int32, (MLA_V, MLA_V), 1)
    eye = (ri == ci).astype(BF16)
    for h in heads:
        _, l, acc = carry[h]
        o_ref[:, vsl[h]] = _dot_tn((acc / l).astype(BF16), eye).astype(o_ref.dtype)


def _mla_prompt_attention(qc, kc, v, nb, seq):
    hp = FLASH_HEADS_PER_STEP
    tq = _tile(seq, FLASH_BLOCK, SUBLANE_BF16)
    nq = seq // tq
    qmap = lambda b, h, i: (b * nq + i, h)
    kmap = lambda b, h, i: (b, h)
    return pl.pallas_call(
        functools.partial(_flash_kernel, tq=tq, hp=hp),
        grid=(nb, MLA_HEADS // hp, nq),
        in_specs=[pl.BlockSpec((tq, hp * MLA_QK_PAD), qmap),
                  pl.BlockSpec((seq, hp * MLA_QK_PAD), kmap),
                  pl.BlockSpec((seq, hp * MLA_V), kmap)],
        out_specs=pl.BlockSpec((tq, hp * MLA_V), qmap),
        out_shape=jax.ShapeDtypeStruct((nb * seq, MLA_HEADS * MLA_V), BF16),
        compiler_params=_cparams(("parallel", "parallel", "arbitrary")),
        name="mla_prompt_attention",
    )(qc, kc, v)


def _absorb_q_kernel(q_ref, g_ref, w_ref, o_ref):
    o_ref[...] = _dot_nt(q_ref[...].astype(F32) * g_ref[...], w_ref[...]).astype(o_ref.dtype)


def _absorb_q(qc, g_k_nope, w_uk2, mp, ms):
    assert mp % ms == 0
    return pl.pallas_call(
        _absorb_q_kernel,
        grid=(MLA_HEADS,),
        in_specs=[pl.BlockSpec((ms, MLA_NOPE), lambda h: (mp // ms, h * (MLA_QK_PAD // MLA_NOPE))),
                  pl.BlockSpec((1, MLA_NOPE), lambda h: (0, 0)),
                  pl.BlockSpec((MLA_KV_RANK, MLA_NOPE), lambda h: (0, h))],
        out_specs=pl.BlockSpec((ms, MLA_KV_RANK), lambda h: (0, h)),
        out_shape=jax.ShapeDtypeStruct((ms, MLA_HEADS * MLA_KV_RANK), BF16),
        compiler_params=_cparams(("parallel",)),
        name="mla_absorb_q",
    )(qc, g_k_nope.reshape(1, -1), w_uk2)


def _expand_o_kernel(o_ref, w_ref, out_ref):
    out_ref[...] = _dot(o_ref[...], w_ref[...]).astype(out_ref.dtype)


def _expand_o(o_lat, w_uv2):
    ms = o_lat.shape[0]
    return pl.pallas_call(
        _expand_o_kernel,
        grid=(MLA_HEADS,),
        in_specs=[pl.BlockSpec((ms, MLA_KV_RANK), lambda h: (0, h)),
                  pl.BlockSpec((MLA_KV_RANK, MLA_V), lambda h: (0, h))],
        out_specs=pl.BlockSpec((ms, MLA_V), lambda h: (0, h)),
        out_shape=jax.ShapeDtypeStruct((ms, MLA_HEADS * MLA_V), BF16),
        compiler_params=_cparams(("parallel",)),
        name="mla_expand_o",
    )(o_lat, w_uv2)


def _softmax_update(sc, values, m_scr, l_scr, acc_scr):
    m_old = m_scr[...]
    m_new = jnp.maximum(m_old, jnp.max(sc, axis=-1, keepdims=True))
    corr = jnp.exp2(m_old - m_new)
    p = jnp.exp2(sc - m_new)
    l_scr[...] = l_scr[...] * corr + jnp.sum(p, axis=-1, keepdims=True)
    acc_scr[...] = acc_scr[...] * corr + _dot(p, values)
    m_scr[...] = m_new


def _sample_attn_kernel(pt_ref, ql_ref, qp_ref, lat_hbm, pe_hbm, ri_hbm, latn_ref, pen_ref, rin_ref, o_ref,
                        m_scr, l_scr, acc_scr, lat_raw, pe_raw, ri_raw, lat_buf, pe_buf, sems,
                        *, li, pps, nsteps, total_steps, steps, page):
    s = pl.program_id(1)
    t = pl.program_id(0) * nsteps + s
    slot = t % 2

    def page_copies(step, buf_slot):
        out = []
        for p in range(pps):
            pg = pt_ref[step * pps + p]
            keys = pl.ds(p * page, page)
            out.append(pltpu.make_async_copy(lat_hbm.at[li, pg], lat_raw.at[buf_slot, keys, :], sems.at[buf_slot, 0]))
            out.append(pltpu.make_async_copy(pe_hbm.at[li, pg], pe_raw.at[buf_slot, :, keys], sems.at[buf_slot, 1]))
            out.append(pltpu.make_async_copy(ri_hbm.at[li, pg], ri_raw.at[buf_slot, :, keys], sems.at[buf_slot, 2]))
        return out

    @pl.when(t == 0)
    def _():
        for c in page_copies(0, 0):
            c.start()

    @pl.when(t + 1 < total_steps)
    def _():
        for c in page_copies(t + 1, 1 - slot):
            c.start()

    @pl.when(s == 0)
    def _():
        m_scr[...] = jnp.full_like(m_scr, -jnp.inf)
        l_scr[...] = jnp.zeros_like(l_scr)
        acc_scr[...] = jnp.zeros_like(acc_scr)

    for c in page_copies(t, slot):
        c.wait()
    lat_buf[...] = lat_raw[slot].astype(BF16)
    pe_buf[...] = pe_raw[slot].astype(BF16)
    ql = ql_ref[...]
    qp = qp_ref[...]
    nsub = SAMPLE_SUBBLOCKS if pps % SAMPLE_SUBBLOCKS == 0 else 1
    width = pps * page // nsub
    subs = [pl.ds(u * width, width) for u in range(nsub)]
    lat = [lat_buf[sl, :] for sl in subs]
    sc = [(_dot_nt(ql, lat[u]) + _dot(qp, pe_buf[:, subs[u]])) * jnp.concatenate([ri_raw[slot, :, subs[u]]] * steps, axis=0)
          for u in range(nsub)]
    for u in range(nsub):
        _softmax_update(sc[u], lat[u], m_scr, l_scr, acc_scr)

    @pl.when(s == nsteps - 1)
    def _():
        latn = latn_ref[...].astype(BF16)
        scn = (_dot_nt(ql, latn) + _dot(qp, pen_ref[...])) * jnp.concatenate([rin_ref[...]] * steps, axis=0)
        qstep = lax.broadcasted_iota(jnp.int32, scn.shape, 0) // MLA_HEADS
        kstep = lax.broadcasted_iota(jnp.int32, scn.shape, 1)
        scn = jnp.where(jnp.logical_and(kstep < steps, kstep <= qstep), scn, -jnp.inf)
        _softmax_update(scn, latn, m_scr, l_scr, acc_scr)
        o_ref[...] = (acc_scr[...] / l_scr[...]).astype(o_ref.dtype)


def _mla_sample_attention(q_lat, q_pe, lat_new, pe_new_t, rinv_new_t, cache_lat, cache_pe_t, cache_rinv_t, li,
                          page_table, steps):
    nb, n_pages = page_table.shape
    page = cache_lat.shape[2]
    pps = _tile(n_pages, PAGES_PER_STEP, 1)
    nsteps = n_pages // pps
    rows = q_lat.shape[1]
    npad = lat_new.shape[1]
    per_b = lambda r, w: pl.BlockSpec((None, r, w), lambda b, s, pt: (b, 0, 0))
    hbm = pl.BlockSpec(memory_space=pl.ANY)
    grid_spec = pltpu.PrefetchScalarGridSpec(
        num_scalar_prefetch=1,
        grid=(nb, nsteps),
        in_specs=[per_b(rows, MLA_KV_RANK), per_b(rows, MLA_ROPE), hbm, hbm, hbm,
                  per_b(npad, MLA_KV_RANK), per_b(MLA_ROPE, npad), per_b(MLA_HEADS, npad)],
        out_specs=per_b(rows, MLA_KV_RANK),
        scratch_shapes=[pltpu.VMEM((rows, 1), F32), pltpu.VMEM((rows, 1), F32),
                        pltpu.VMEM((rows, MLA_KV_RANK), F32),
                        pltpu.VMEM((2, pps * page, MLA_KV_RANK), F32),
                        pltpu.VMEM((2, MLA_ROPE, pps * page), F32),
                        pltpu.VMEM((2, MLA_HEADS, pps * page), F32),
                        pltpu.VMEM((pps * page, MLA_KV_RANK), BF16),
                        pltpu.VMEM((MLA_ROPE, pps * page), BF16),
                        pltpu.SemaphoreType.DMA((2, 3))])
    return pl.pallas_call(
        functools.partial(_sample_attn_kernel, li=li, pps=pps, nsteps=nsteps, total_steps=nb * nsteps, steps=steps,
                          page=page),
        grid_spec=grid_spec,
        out_shape=jax.ShapeDtypeStruct((nb, rows, MLA_KV_RANK), BF16),
        compiler_params=_cparams(("arbitrary", "arbitrary")),
        name="mla_sample_attention",
    )(page_table.reshape(-1), q_lat, q_pe, cache_lat, cache_pe_t, cache_rinv_t, lat_new, pe_new_t.astype(BF16),
      rinv_new_t)


def _rope_tables(pos):
    half = MLA_ROPE // 2
    inv = ROPE_THETA ** (-jnp.arange(half, dtype=F32) / half)
    ang = pos.astype(F32)[:, None] * inv[None, :]
    cos, sin = jnp.cos(ang), jnp.sin(ang)
    return jnp.concatenate([cos, cos], axis=-1), jnp.concatenate([-sin, sin], axis=-1)


def _mla_mixer(h, x_res, cache_lat, cache_pe, cache_rinv, page_table, w_in, q_norm_g, kv_norm_g, w_uq, w_uk,
               w_uv, g_q, g_k, w_o, li, dims, past_len):
    mp, ms, nbp, seq, nbs, steps = dims
    pos = jnp.concatenate([jnp.tile(jnp.arange(seq), nbp), jnp.tile(past_len + jnp.arange(steps), nbs)])
    cos_t, sin_t = _rope_tables(pos)
    proj = _mm(h, w_in, (li,), name="mla_in_proj")
    w_uq3 = w_uq[li].reshape(MLA_Q_RANK, MLA_HEADS, MLA_QK)
    qc, qr = _mla_q(proj, q_norm_g[li], w_uq3[:, :, :MLA_NOPE].reshape(MLA_Q_RANK, -1),
                    w_uq3[:, :, MLA_NOPE:].reshape(MLA_Q_RANK, -1), g_q[li], cos_t, sin_t)
    w_uk2 = w_uk[li].reshape(MLA_KV_RANK, -1)
    w_uv2 = w_uv[li].reshape(MLA_KV_RANK, -1)
    lat, kpe, rinv, kc, v = _mla_kv(proj, proj[:, MLA_Q_RANK + MLA_KV_RANK:], kv_norm_g[li], w_uk2, w_uv2,
                                         g_k[li], cos_t, sin_t)
    o_p = _mla_prompt_attention(qc, kc, v, nbp, seq)

    q_lat = _absorb_q(qc, g_k[li, :MLA_NOPE], w_uk2, mp, ms)
    padn = lambda a: jnp.pad(a[mp:].reshape(nbs, steps, -1), ((0, 0), (0, 8 - steps), (0, 0)))
    o_lat = _mla_sample_attention(q_lat.reshape(nbs, steps * MLA_HEADS, MLA_KV_RANK),
                                  qr[mp:].reshape(nbs, steps * MLA_HEADS, MLA_ROPE),
                                  padn(lat), padn(kpe).swapaxes(1, 2), padn(rinv).swapaxes(1, 2), cache_lat,
                                  cache_pe.swapaxes(2, 3), cache_rinv.swapaxes(2, 3), li, page_table, steps)
    o_s = _expand_o(o_lat.reshape(ms, MLA_HEADS * MLA_KV_RANK), w_uv2)
    x_new = _mm(jnp.concatenate([o_p, o_s], axis=0), w_o, (li,), residual=x_res, tk=2048, name="mla_out_proj")
    return x_new, lat, kpe, rinv


def kernel(x_prompt, x_sample, state_gdn_ssm, state_gdn_conv, cache_mla_latent, cache_mla_k_pe, cache_mla_k_rinv,
           page_table, norm_g, ffn_w_gate, ffn_w_up, ffn_w_down, gdn_w_in, gdn_conv_w, gdn_a_log, gdn_dt_bias,
           gdn_norm_g, gdn_w_out, mla_w_in, mla_q_norm_g, mla_kv_norm_g, mla_w_uq, mla_w_uk, mla_w_uv,
           mla_qk_norm_q, mla_qk_norm_k, mla_w_o):
    nbp, seq, d = x_prompt.shape
    nbs, steps, _ = x_sample.shape
    mp, ms = nbp * seq, nbs * steps
    dims = (mp, ms, nbp, seq, nbs, steps)
    depth = norm_g.shape[0]
    past_len = page_table.shape[1] * cache_mla_latent.shape[2]
    x = jnp.concatenate([x_prompt.reshape(mp, d), x_sample.reshape(ms, d)], axis=0)
    w_down_bf16 = ffn_w_down.astype(BF16)
    h = _norm(x, norm_g[0, 0], BF16)
    outs = {k: [] for k in ("ssm_p", "conv_p", "ssm_s", "conv_s", "lat", "pe", "rinv")}
    for layer in range(depth):
        li = layer // 2
        x = _macaron_half(x, h, ffn_w_gate, ffn_w_up, w_down_bf16, layer, 0)
        h = _norm(x, norm_g[layer, 1], BF16)
        if layer % 2 == 0:
            x, ssm_p, conv_p, ssm_s, conv_s = _gdn_mixer(
                h, x, state_gdn_ssm, state_gdn_conv, gdn_w_in, gdn_conv_w, gdn_a_log, gdn_dt_bias, gdn_norm_g,
                gdn_w_out, li, dims)
            for key, val in (("ssm_p", ssm_p), ("conv_p", conv_p), ("ssm_s", ssm_s), ("conv_s", conv_s)):
                outs[key].append(val)
        else:
            x, lat, kpe, rinv = _mla_mixer(
                h, x, cache_mla_latent, cache_mla_k_pe, cache_mla_k_rinv, page_table, mla_w_in, mla_q_norm_g,
                mla_kv_norm_g, mla_w_uq, mla_w_uk, mla_w_uv, mla_qk_norm_q, mla_qk_norm_k, mla_w_o, li, dims,
                past_len)
            for key, val in (("lat", lat), ("pe", kpe), ("rinv", rinv)):
                outs[key].append(val)
        h = _norm(x, norm_g[layer, 2], BF16)
        x = _macaron_half(x, h, ffn_w_gate, ffn_w_up, w_down_bf16, layer, 1)
        if layer + 1 < depth:
            x, h = _norm2(x, norm_g[layer, 3], norm_g[layer + 1, 0])
        else:
            y_p = _norm(x, norm_g[layer, 3], F32, 0, mp).reshape(nbp, seq, d)
            y_s = _norm(x, norm_g[layer, 3], F32, mp, ms).reshape(nbs, steps, d)

    def split(a):
        return a[:mp].reshape(nbp, seq, -1), a[mp:].reshape(nbs, steps, -1)

    lat_p, lat_s = zip(*[split(a) for a in outs["lat"]])
    pe_p, pe_s = zip(*[split(a) for a in outs["pe"]])
    rinv_p, rinv_s = zip(*[split(a) for a in outs["rinv"]])
    return (y_p, y_s,
            jnp.stack(outs["ssm_p"]), jnp.stack(outs["conv_p"]), jnp.stack(lat_p), jnp.stack(pe_p),
            jnp.stack(rinv_p),
            jnp.stack(outs["ssm_s"]), jnp.stack(outs["conv_s"]), jnp.stack(lat_s), jnp.stack(pe_s),
            jnp.stack(rinv_s))
```

```python
import functools
import math

import jax
import jax.numpy as jnp
from jax import lax
from jax.experimental import pallas as pl
from jax.experimental.pallas import tpu as pltpu

F32 = jnp.float32
BF16 = jnp.bfloat16
EPS = 1e-6
ROPE_THETA = 10000.0

GDN_K_HEADS = 16
GDN_V_HEADS = 32
GDN_HEAD = 128
GDN_CONV = 4
GDN_CHUNK = 64
MLA_HEADS = 64
MLA_Q_RANK = 1024
MLA_KV_RANK = 512
MLA_NOPE = 128
MLA_ROPE = 64
MLA_V = 128
MLA_QK = MLA_NOPE + MLA_ROPE
MLA_QK_PAD = 256

LANE = 128
SUBLANE_BF16 = 16
VMEM_LIMIT = 56 * 1024 * 1024

FFN_DOWN_K_TILE = 3584
FFN_UP_ROW_TILE = 2176
ROW_TILE = 1088
GDN_BLOCK_ROWS = 256
GDN_GROUPS_PER_STEP = 8
MLA_Q_HEADS_PER_STEP = 8
MLA_KV_HEADS_PER_STEP = 8
FLASH_HEADS_PER_STEP = 8
FLASH_BLOCK = 512
PAGES_PER_STEP = 32
SAMPLE_SUBBLOCKS = 8
SAMPLE_PAD = 16


def _tile(n, target, align):
    if n <= target:
        return n
    t = (target // align) * align
    while t >= align:
        if n % t == 0:
            return t
        t -= align
    return n


def _cparams(sem):
    return pltpu.CompilerParams(dimension_semantics=sem, vmem_limit_bytes=VMEM_LIMIT)


def _dot(a, b):
    return jnp.dot(a.astype(BF16), b.astype(BF16), preferred_element_type=F32)


def _dot_nt(a, b):
    return lax.dot_general(a.astype(BF16), b.astype(BF16), (((1,), (1,)), ((), ())),
                           preferred_element_type=F32)


def _dot_tn(a, b):
    return lax.dot_general(a.astype(BF16), b.astype(BF16), (((0,), (0,)), ((), ())),
                           preferred_element_type=F32)


def _split3(x):
    x1 = x.astype(BF16)
    r1 = x - x1.astype(F32)
    x2 = r1.astype(BF16)
    x3 = (r1 - x2.astype(F32)).astype(BF16)
    return x1, x2, x3


def _rms(x, g):
    return x * lax.rsqrt(jnp.mean(x * x, axis=-1, keepdims=True) + EPS) * g


def _sigmoid(x):
    return 1.0 / (1.0 + jnp.exp(-x))


def _norm_kernel(x_ref, g_ref, o_ref):
    o_ref[...] = _rms(x_ref[...], g_ref[...]).astype(o_ref.dtype)


def _norm2_kernel(x_ref, g1_ref, g2_ref, y_ref, h_ref):
    y = _rms(x_ref[...], g1_ref[...])
    y_ref[...] = y
    h_ref[...] = _rms(y, g2_ref[...]).astype(h_ref.dtype)


def _norm(x, g, out_dtype, row0=0, nrows=None):
    d = x.shape[1]
    nrows = x.shape[0] if nrows is None else nrows
    tm = _tile(math.gcd(row0, nrows), 512, SUBLANE_BF16)
    return pl.pallas_call(
        _norm_kernel,
        grid=(nrows // tm,),
        in_specs=[pl.BlockSpec((tm, d), lambda i: (i + row0 // tm, 0)), pl.BlockSpec((1, d), lambda i: (0, 0))],
        out_specs=pl.BlockSpec((tm, d), lambda i: (i, 0)),
        out_shape=jax.ShapeDtypeStruct((nrows, d), out_dtype),
        compiler_params=_cparams(("parallel",)),
        name="rmsnorm",
    )(x, g.reshape(1, d))


def _norm2(x, g1, g2):
    m, d = x.shape
    tm = _tile(m, 512, SUBLANE_BF16)
    return pl.pallas_call(
        _norm2_kernel,
        grid=(m // tm,),
        in_specs=[pl.BlockSpec((tm, d), lambda i: (i, 0)), pl.BlockSpec((1, d), lambda i: (0, 0)),
                  pl.BlockSpec((1, d), lambda i: (0, 0))],
        out_specs=[pl.BlockSpec((tm, d), lambda i: (i, 0)), pl.BlockSpec((tm, d), lambda i: (i, 0))],
        out_shape=[jax.ShapeDtypeStruct((m, d), F32), jax.ShapeDtypeStruct((m, d), BF16)],
        compiler_params=_cparams(("parallel",)),
        name="rmsnorm_pair",
    )(x, g1.reshape(1, d), g2.reshape(1, d))


def _mm_kernel(*refs, has_res, nk, w_is_nk):
    if has_res:
        a_ref, w_ref, r_ref, o_ref = refs
    else:
        a_ref, w_ref, o_ref = refs
    part = (_dot_nt if w_is_nk else _dot)(a_ref[...], w_ref[...])
    if nk == 1:
        if has_res:
            part = part + r_ref[...]
        o_ref[...] = part.astype(o_ref.dtype)
        return
    k = pl.program_id(2)

    @pl.when(k == 0)
    def _():
        o_ref[...] = part + r_ref[...] if has_res else part

    @pl.when(k > 0)
    def _():
        o_ref[...] += part


def _mm(a, w, wprefix=(), *, n_out=None, out_dtype=F32, residual=None, tm=ROW_TILE, tn=1024, tk=1024,
        w_is_nk=False, a_resident=False, name="matmul"):
    m, kdim = a.shape
    assert w.shape[-1 if w_is_nk else -2] == kdim
    n = n_out or w.shape[-2 if w_is_nk else -1]
    tm = _tile(m, tm, SUBLANE_BF16)
    tn = _tile(n, tn, LANE)
    tk = _tile(kdim, tk, LANE)
    nk = kdim // tk
    assert nk == 1 or out_dtype == F32
    npre = len(wprefix)
    assert not a_resident or nk == 1
    in_specs = [pl.BlockSpec((tm, tk), lambda i, j, k: (i, k), **({"pipeline_mode": pl.Buffered(1)} if a_resident else {})),
                pl.BlockSpec((None,) * npre + (tn, tk), lambda i, j, k: tuple(wprefix) + (j, k)) if w_is_nk else
                pl.BlockSpec((None,) * npre + (tk, tn), lambda i, j, k: tuple(wprefix) + (k, j))]
    args = [a, w]
    if residual is not None:
        in_specs.append(pl.BlockSpec((tm, tn), lambda i, j, k: (i, j)))
        args.append(residual)
    return pl.pallas_call(
        functools.partial(_mm_kernel, has_res=residual is not None, nk=nk, w_is_nk=w_is_nk),
        grid=(m // tm, n // tn, nk),
        in_specs=in_specs,
        out_specs=pl.BlockSpec((tm, tn), lambda i, j, k: (i, j)),
        out_shape=jax.ShapeDtypeStruct((m, n), out_dtype),
        compiler_params=_cparams(("parallel", "parallel", "arbitrary")),
        name=name,
    )(*args)


def _ffn_up_kernel(h_ref, wg_ref, wu_ref, o_ref):
    h = h_ref[...]
    gate = _dot(h, wg_ref[...])
    up = _dot(h, wu_ref[...])
    o_ref[...] = (0.5 * gate * _sigmoid(gate) * up).astype(o_ref.dtype)


def _ffn_up(h, w_gate, w_up, layer, half):
    m, d = h.shape
    f_dim = w_gate.shape[-1]
    tm = _tile(m, FFN_UP_ROW_TILE, SUBLANE_BF16)
    tf = _tile(f_dim, 256, LANE)
    wspec = pl.BlockSpec((None, None, d, tf), lambda i, f: (layer, half, 0, f))
    return pl.pallas_call(
        _ffn_up_kernel,
        grid=(m // tm, f_dim // tf),
        in_specs=[pl.BlockSpec((tm, d), lambda i, f: (i, 0), pipeline_mode=pl.Buffered(1)), wspec, wspec],
        out_specs=pl.BlockSpec((tm, tf), lambda i, f: (i, f)),
        out_shape=jax.ShapeDtypeStruct((m, f_dim), BF16),
        compiler_params=_cparams(("parallel", "arbitrary")),
        name="ffn_up",
    )(h, w_gate, w_up)


def _split_contraction(kdim, target):
    tk = _tile(kdim, target, LANE)
    if kdim // tk <= 8:
        return tk, kdim // tk, 0
    for tail in (2 * LANE, LANE):
        main = kdim - tail
        tk = _tile(main, target, LANE)
        if main % tail == 0 and main // tk <= 8:
            return tk, main // tk, tail
    raise ValueError(f"no contraction tiling for {kdim}")


def _ffn_down_kernel(a_ref, w_ref, at_ref, wt_ref, r_ref, o_ref):
    k = pl.program_id(2)
    part = _dot(a_ref[...], w_ref[...])

    @pl.when(k == 0)
    def _():
        o_ref[...] = r_ref[...] + part + _dot(at_ref[...], wt_ref[...])

    @pl.when(k > 0)
    def _():
        o_ref[...] += part


def _macaron_half(x, h, w_gate, w_up, w_down_bf16, layer, half):
    act = _ffn_up(h, w_gate, w_up, layer, half)
    m, f_dim = act.shape
    n = w_down_bf16.shape[-1]
    tk, nk, tail = _split_contraction(f_dim, FFN_DOWN_K_TILE)
    if tail == 0:
        return _mm(act, w_down_bf16, (layer, half), residual=x, tk=tk, name="ffn_down")
    tm = _tile(m, ROW_TILE, SUBLANE_BF16)
    tn = _tile(n, 1024, LANE)
    tail_blk = nk * tk // tail
    return pl.pallas_call(
        _ffn_down_kernel,
        grid=(m // tm, n // tn, nk),
        in_specs=[pl.BlockSpec((tm, tk), lambda i, j, k: (i, k)),
                  pl.BlockSpec((None, None, tk, tn), lambda i, j, k: (layer, half, k, j)),
                  pl.BlockSpec((tm, tail), lambda i, j, k: (i, tail_blk)),
                  pl.BlockSpec((None, None, tail, tn), lambda i, j, k: (layer, half, tail_blk, j)),
                  pl.BlockSpec((tm, tn), lambda i, j, k: (i, j))],
        out_specs=pl.BlockSpec((tm, tn), lambda i, j, k: (i, j)),
        out_shape=jax.ShapeDtypeStruct((m, n), F32),
        compiler_params=_cparams(("parallel", "parallel", "arbitrary")),
        name="ffn_down",
    )(act, w_down_bf16, act, w_down_bf16, x)


def _gates_kernel(b_ref, a_ref, alog_ref, dt_ref, beta_ref, g_ref):
    beta_ref[...] = _sigmoid(b_ref[...])
    x = a_ref[...] + dt_ref[...]
    softplus = jnp.maximum(x, 0.0) + jnp.log1p(jnp.exp(-jnp.abs(x)))
    g_ref[...] = -jnp.exp(alog_ref[...]) * softplus


def _gdn_gates(beta_in, a_in, a_log, dt_bias):
    m, h = beta_in.shape
    tm = _tile(m, 2176, 8)
    row = pl.BlockSpec((tm, h), lambda i: (i, 0))
    vec = pl.BlockSpec((1, h), lambda i: (0, 0))
    return pl.pallas_call(
        _gates_kernel,
        grid=(m // tm,),
        in_specs=[row, row, vec, vec],
        out_specs=[row, row],
        out_shape=[jax.ShapeDtypeStruct((m, h), F32)] * 2,
        compiler_params=_cparams(("parallel",)),
        name="gdn_gates",
    )(beta_in, a_in, a_log.reshape(1, h), dt_bias.reshape(1, h))


def _conv_finish(y, o_ref, row_sl, is_qk, qk_scale):
    y = y * _sigmoid(y)
    tc = y.shape[-1]

    @pl.when(is_qk)
    def _():
        for h in range(tc // GDN_HEAD):
            sl = slice(h * GDN_HEAD, (h + 1) * GDN_HEAD)
            yh = y[:, sl]
            r = lax.rsqrt(jnp.sum(yh * yh, axis=-1, keepdims=True) + EPS)
            o_ref[row_sl + (sl,)] = (yh * (r * qk_scale)).astype(o_ref.dtype)

    @pl.when(jnp.logical_not(is_qk))
    def _():
        o_ref[row_sl + (slice(None),)] = y.astype(o_ref.dtype)


def _conv_prompt_kernel(x_ref, halo_ref, w_ref, o_ref, *, tiles_per_seq, nq_tiles, nqk_tiles):
    i = pl.program_id(0)
    c = pl.program_id(1)
    x = x_ref[...]
    tt = x.shape[0]
    halo = jnp.where(i % tiles_per_seq == 0, 0.0, halo_ref[...])
    xp = jnp.concatenate([halo, x], axis=0)
    w = w_ref[...]
    hl = halo.shape[0]
    y = pltpu.roll(xp, GDN_CONV - 1, 0)[hl:] * w[0:1]
    for j in range(1, GDN_CONV - 1):
        y = y + pltpu.roll(xp, GDN_CONV - 1 - j, 0)[hl:] * w[j:j + 1]
    y = y + x * w[GDN_CONV - 1:GDN_CONV]
    qk_scale = jnp.where(c < nq_tiles, GDN_HEAD ** -0.5, 1.0)
    _conv_finish(y, o_ref, (slice(None),), c < nqk_tiles, qk_scale)


def _conv_sample_kernel(x_ref, w_ref, o_ref, *, steps, nq_tiles, nqk_tiles):
    c = pl.program_id(0)
    w = w_ref[...]
    qk_scale = jnp.where(c < nq_tiles, GDN_HEAD ** -0.5, 1.0)
    for s in range(steps):
        y = x_ref[s] * w[0:1]
        for j in range(1, GDN_CONV):
            y = y + x_ref[s + j] * w[j:j + 1]
        _conv_finish(y, o_ref, (s, slice(None)), c < nqk_tiles, qk_scale)


def _gdn_conv_prompt(qkvz, conv_w, mp, seq):
    cdim = conv_w.shape[-1]
    tt = _tile(seq, 256, SUBLANE_BF16)
    tc = _tile(2 * GDN_K_HEADS * GDN_HEAD // 2, 512, GDN_HEAD)
    nq = GDN_K_HEADS * GDN_HEAD // tc
    return pl.pallas_call(
        functools.partial(_conv_prompt_kernel, tiles_per_seq=seq // tt, nq_tiles=nq, nqk_tiles=2 * nq),
        grid=(mp // tt, cdim // tc),
        in_specs=[pl.BlockSpec((tt, tc), lambda i, c: (i, c)),
                  pl.BlockSpec((SUBLANE_BF16, tc), lambda i, c: (jnp.maximum(i * (tt // SUBLANE_BF16) - 1, 0), c)),
                  pl.BlockSpec((GDN_CONV, tc), lambda i, c: (0, c))],
        out_specs=pl.BlockSpec((tt, tc), lambda i, c: (i, c)),
        out_shape=jax.ShapeDtypeStruct((mp, cdim), BF16),
        compiler_params=_cparams(("parallel", "parallel")),
        name="gdn_conv_prompt",
    )(qkvz, qkvz, conv_w)


def _gdn_conv_sample(xpad_t, conv_w, steps):
    tp, nb, cdim = xpad_t.shape
    tc = _tile(GDN_K_HEADS * GDN_HEAD, 512, GDN_HEAD)
    nq = GDN_K_HEADS * GDN_HEAD // tc
    return pl.pallas_call(
        functools.partial(_conv_sample_kernel, steps=steps, nq_tiles=nq, nqk_tiles=2 * nq),
        grid=(cdim // tc,),
        in_specs=[pl.BlockSpec((tp, nb, tc), lambda c: (0, 0, c)),
                  pl.BlockSpec((GDN_CONV, tc), lambda c: (0, c))],
        out_specs=pl.BlockSpec((steps, nb, tc), lambda c: (0, 0, c)),
        out_shape=jax.ShapeDtypeStruct((steps, nb, cdim), BF16),
        compiler_params=_cparams(("parallel",)),
        name="gdn_conv_sample",
    )(xpad_t, conv_w)


def _gdn_chunk_kernel(*refs, c, hb, ngroups, rep, nchunks, has_s0):
    if has_s0:
        q_ref, k_ref, v_ref, z_ref, g_ref, b_ref, ng_ref, s0_ref, o_ref, sfin_ref, s_scr = refs
    else:
        q_ref, k_ref, v_ref, z_ref, g_ref, b_ref, ng_ref, o_ref, sfin_ref, s_scr = refs
    n = pl.program_id(2)

    @pl.when(n == 0)
    def _():
        s_scr[...] = s0_ref[...] if has_s0 else jnp.zeros_like(s_scr)

    d = GDN_HEAD
    r = hb * c
    rowi = lax.broadcasted_iota(jnp.int32, (r, r), 0)
    coli = lax.broadcasted_iota(jnp.int32, (r, r), 1)
    same = (rowi ^ coli) < c
    tril = jnp.logical_and(same, rowi >= coli)
    strict = jnp.logical_and(same, rowi > coli)
    triu = jnp.logical_and(same, rowi <= coli)
    eye = rowi == coli
    eye_f = eye.astype(F32)
    head_of_row = lax.broadcasted_iota(jnp.int32, (r, hb), 0) // c == lax.broadcasted_iota(jnp.int32, (r, hb), 1)
    ng = ng_ref[...]

    def stack(ref, heads):
        return jnp.concatenate([ref[:, h * d:(h + 1) * d] for h in heads], axis=0)

    def stacked_column(ref, j0):
        tiled = jnp.concatenate([ref[:, j0:j0 + hb]] * hb, axis=0)
        return jnp.sum(jnp.where(head_of_row, tiled, 0.0), axis=1, keepdims=True)

    groups = range(ngroups)
    vheads = [list(range(gi * hb, (gi + 1) * hb)) for gi in groups]
    qst = [stack(q_ref, [h // rep for h in vheads[gi]]) for gi in groups]
    kst = [stack(k_ref, [h // rep for h in vheads[gi]]) for gi in groups]
    kf = [kst[gi].astype(F32) for gi in groups]
    vf = [stack(v_ref, vheads[gi]).astype(F32) for gi in groups]
    g = [stacked_column(g_ref, gi * hb) for gi in groups]
    beta = [stacked_column(b_ref, gi * hb) for gi in groups]
    qk_kk = [_dot_nt(jnp.concatenate([qst[gi], kst[gi]], axis=0), kst[gi]) for gi in groups]
    g_row = [jnp.sum(jnp.where(eye, g[gi], 0.0), axis=0, keepdims=True) for gi in groups]
    gc_col = [jnp.sum(jnp.where(tril, g_row[gi], 0.0), axis=1, keepdims=True) for gi in groups]
    gc_row = [jnp.sum(jnp.where(triu, g[gi], 0.0), axis=0, keepdims=True) for gi in groups]
    decay = [jnp.where(tril, jnp.exp(jnp.where(tril, gc_col[gi] - gc_row[gi], 0.0)), 0.0) for gi in groups]
    qk = [qk_kk[gi][:r] * decay[gi] for gi in groups]
    mm = [-jnp.where(strict, qk_kk[gi][r:] * beta[gi] * decay[gi], 0.0) for gi in groups]
    inv = [eye_f + mm[gi] for gi in groups]
    for _ in range(int(math.log2(c)) - 1):
        mm = [_dot(mm[gi], mm[gi]) for gi in groups]
        inv = [inv[gi] + _dot(inv[gi], mm[gi]) for gi in groups]
    e_col = [jnp.exp(gc_col[gi]) for gi in groups]
    uw = [_dot(inv[gi], jnp.concatenate([vf[gi] * beta[gi], kf[gi] * (beta[gi] * e_col[gi])], axis=1))
          for gi in groups]
    qd = [qst[gi].astype(F32) * e_col[gi] for gi in groups]
    heads = [(gi, j) for gi in groups for j in range(hb)]
    rows = [slice(j * c, (j + 1) * c) for j in range(hb)]
    s_old = {(gi, j): s_scr[gi * hb + j] for gi, j in heads}
    wq = {(gi, j): _dot(jnp.concatenate([uw[gi][rows[j], d:], qd[gi][rows[j]]], axis=0), s_old[gi, j])
          for gi, j in heads}
    v_new = {(gi, j): uw[gi][rows[j], :d] - wq[gi, j][:c] for gi, j in heads}
    for gi, j in heads:
        gc_last = gc_col[gi][(j + 1) * c - 1:(j + 1) * c, :]
        k_dec = kf[gi][rows[j]] * jnp.exp(gc_last - gc_col[gi][rows[j]])
        s_scr[gi * hb + j] = s_old[gi, j] * jnp.exp(gc_last) + _dot_tn(k_dec, v_new[gi, j])
    o = [jnp.concatenate([wq[gi, j][c:] for j in range(hb)], axis=0)
         + _dot(qk[gi], jnp.concatenate([v_new[gi, j] for j in range(hb)], axis=0)) for gi in groups]
    for gi in groups:
        z = stack(z_ref, vheads[gi])
        out = (_rms(o[gi], ng) * (z * _sigmoid(z))).astype(o_ref.dtype)
        for j in range(hb):
            o_ref[:, (gi * hb + j) * d:(gi * hb + j + 1) * d] = out[rows[j]]

    @pl.when(n == nchunks - 1)
    def _():
        sfin_ref[...] = s_scr[...]


def _gdn_rule(qkv, z, z_col0, g, beta, norm_g, s0, nb, seq, c):
    d = GDN_HEAD
    hb = min(GDN_BLOCK_ROWS // c, GDN_V_HEADS)
    ngroups = min(GDN_GROUPS_PER_STEP, GDN_V_HEADS // hb)
    hs = hb * ngroups
    rep = GDN_V_HEADS // GDN_K_HEADS
    nchunks = seq // c
    koff = GDN_K_HEADS // (hs // rep)
    voff = 2 * GDN_K_HEADS // hs
    zoff = z_col0 // (hs * d)
    rowmap = lambda b, h, n: (b * nchunks + n, h)
    gatemap = lambda b, h, n: (h, b * nchunks + n, 0)
    in_specs = [pl.BlockSpec((c, hs // rep * d), rowmap),
                pl.BlockSpec((c, hs // rep * d), lambda b, h, n: (b * nchunks + n, koff + h)),
                pl.BlockSpec((c, hs * d), lambda b, h, n: (b * nchunks + n, voff + h)),
                pl.BlockSpec((c, hs * d), lambda b, h, n: (b * nchunks + n, zoff + h)),
                pl.BlockSpec((None, c, hs), gatemap),
                pl.BlockSpec((None, c, hs), gatemap),
                pl.BlockSpec((1, d), lambda b, h, n: (0, 0))]
    args = [qkv, qkv, qkv, z, _head_groups(g, hs), _head_groups(beta, hs), norm_g.reshape(1, d)]
    if s0 is not None:
        in_specs.append(pl.BlockSpec((None, hs, d, d), lambda b, h, n: (b, h, 0, 0)))
        args.append(s0)
    return pl.pallas_call(
        functools.partial(_gdn_chunk_kernel, c=c, hb=hb, ngroups=ngroups, rep=rep, nchunks=nchunks,
                          has_s0=s0 is not None),
        grid=(nb, GDN_V_HEADS // hs, nchunks),
        in_specs=in_specs,
        out_specs=[pl.BlockSpec((c, hs * d), rowmap),
                   pl.BlockSpec((None, hs, d, d), lambda b, h, n: (b, h, 0, 0))],
        out_shape=[jax.ShapeDtypeStruct((nb * seq, GDN_V_HEADS * d), BF16),
                   jax.ShapeDtypeStruct((nb, GDN_V_HEADS, d, d), F32)],
        scratch_shapes=[pltpu.VMEM((hs, d, d), F32)],
        compiler_params=_cparams(("parallel", "parallel", "arbitrary")),
        name="gdn_rule",
    )(*args)


def _head_groups(x, hg):
    r, h = x.shape
    return x.reshape(r, h // hg, hg).transpose(1, 0, 2)


def _pad_steps(x, nb, steps, pad_to):
    cdim = x.shape[-1]
    x = x.reshape(nb, steps, cdim)
    return jnp.pad(x, ((0, 0), (0, pad_to - steps), (0, 0))).reshape(nb * pad_to, cdim)


def _gdn_mixer(h, x_res, state_ssm, state_conv, w_in, conv_w, a_log, dt_bias, norm_g, w_out, li, dims):
    mp, ms, nbp, seq, nbs, steps = dims
    cdim = conv_w.shape[-1]
    vdim = GDN_V_HEADS * GDN_HEAD
    w_in_t = jnp.swapaxes(w_in, 1, 2)
    qkvz = _mm(h, w_in_t, (li,), n_out=cdim + vdim, tm=FFN_UP_ROW_TILE, tn=512, tk=h.shape[1], w_is_nk=True,
               a_resident=True, name="gdn_in_proj")
    ba = _mm(h, w_in_t[li, cdim + vdim:], w_is_nk=True, name="gdn_gate_proj")
    beta, g = _gdn_gates(ba[:, :GDN_V_HEADS], ba[:, GDN_V_HEADS:], a_log[li], dt_bias[li])

    qkv_p = _gdn_conv_prompt(qkvz, conv_w[li], mp, seq)
    o_p, ssm_p = _gdn_rule(qkv_p, qkvz, cdim, g[:mp], beta[:mp], norm_g[li], None, nbp, seq,
                           GDN_CHUNK if seq % GDN_CHUNK == 0 else seq)
    conv_p = jnp.stack([qkvz[(b + 1) * seq - (GDN_CONV - 1):(b + 1) * seq, :cdim] for b in range(nbp)])

    qkv_s = qkvz[mp:, :cdim].reshape(nbs, steps, cdim)
    win = jnp.concatenate([state_conv[li], qkv_s], axis=1)
    conv_s = win[:, win.shape[1] - (GDN_CONV - 1):]
    win_t = jnp.pad(win, ((0, 0), (0, 8 - win.shape[1] % 8 if win.shape[1] % 8 else 0), (0, 0))).transpose(1, 0, 2)
    qkv_c = _gdn_conv_sample(win_t, conv_w[li], steps).transpose(1, 0, 2).reshape(ms, cdim)
    pad = lambda a: _pad_steps(a, nbs, steps, SAMPLE_PAD)
    o_s, ssm_s = _gdn_rule(pad(qkv_c), pad(qkvz[mp:, cdim:cdim + vdim]), 0, pad(g[mp:]), pad(beta[mp:]),
                           norm_g[li], state_ssm[li], nbs, SAMPLE_PAD, SAMPLE_PAD)
    o_s = o_s.reshape(nbs, SAMPLE_PAD, vdim)[:, :steps].reshape(ms, vdim)
    o_all = jnp.concatenate([o_p, o_s], axis=0)
    x_new = _mm(o_all, w_out, (li,), residual=x_res, tk=2048, name="gdn_out_proj")
    return x_new, ssm_p, conv_p, ssm_s, conv_s


def _mla_q_kernel(cq_ref, gq_ref, wn_ref, wr_ref, gn_ref, gr_ref, cos_ref, sin_ref, en_ref, er_ref, qc_ref, qr_ref,
                  cqn_scr, *, hg, scale):
    j = pl.program_id(1)

    @pl.when(j == 0)
    def _():
        cqn_scr[...] = _rms(cq_ref[...], gq_ref[...]).astype(BF16)

    a = cqn_scr[...]
    qn = _dot(a, wn_ref[...])
    qr = _dot(a, wr_ref[...])
    width = hg * MLA_ROPE
    half = MLA_ROPE // 2
    xg = qr * gr_ref[...]
    lane = lax.broadcasted_iota(jnp.int32, xg.shape, 1)
    swapped = jnp.where(lane % MLA_ROPE < half, pltpu.roll(xg, width - half, 1), pltpu.roll(xg, half, 1))
    roped = xg * cos_ref[...] + swapped * sin_ref[...]
    gn = gn_ref[...]
    ss = _dot(qn * qn, en_ref[...]) + _dot(qr * qr, er_ref[...])
    r_all = lax.rsqrt(ss / MLA_QK + EPS) * scale
    for h in range(hg):
        ns = slice(h * MLA_NOPE, (h + 1) * MLA_NOPE)
        rs = slice(h * MLA_ROPE, (h + 1) * MLA_ROPE)
        an = qn[:, ns]
        r = r_all[:, h:h + 1]
        q_rope = (roped[:, rs] * r).astype(qr_ref.dtype)
        base = h * MLA_QK_PAD
        qc_ref[:, base:base + MLA_NOPE] = (an * r * gn).astype(qc_ref.dtype)
        qc_ref[:, base + MLA_NOPE:base + MLA_QK] = q_rope
        qc_ref[:, base + MLA_QK:base + MLA_QK_PAD] = jnp.zeros((an.shape[0], MLA_QK_PAD - MLA_QK), qc_ref.dtype)
        qr_ref[:, rs] = q_rope


def _mla_q(proj, q_norm_g, w_uq_nope, w_uq_rope, g_q, cos_t, sin_t):
    m = proj.shape[0]
    hg = MLA_Q_HEADS_PER_STEP
    tm = _tile(m, ROW_TILE, SUBLANE_BF16)
    gr_t = jnp.tile(g_q[MLA_NOPE:], hg).reshape(1, hg * MLA_ROPE)
    head_of = lambda width: (jnp.arange(hg * width)[:, None] // width == jnp.arange(hg)[None, :]).astype(BF16)
    return pl.pallas_call(
        functools.partial(_mla_q_kernel, hg=hg, scale=MLA_QK ** -0.5 * math.log2(math.e)),
        grid=(m // tm, MLA_HEADS // hg),
        in_specs=[pl.BlockSpec((tm, MLA_Q_RANK), lambda i, j: (i, 0)),
                  pl.BlockSpec((1, MLA_Q_RANK), lambda i, j: (0, 0)),
                  pl.BlockSpec((MLA_Q_RANK, hg * MLA_NOPE), lambda i, j: (0, j)),
                  pl.BlockSpec((MLA_Q_RANK, hg * MLA_ROPE), lambda i, j: (0, j)),
                  pl.BlockSpec((1, MLA_NOPE), lambda i, j: (0, 0)),
                  pl.BlockSpec((1, hg * MLA_ROPE), lambda i, j: (0, 0)),
                  pl.BlockSpec((tm, hg * MLA_ROPE), lambda i, j: (i, 0)),
                  pl.BlockSpec((tm, hg * MLA_ROPE), lambda i, j: (i, 0)),
                  pl.BlockSpec((hg * MLA_NOPE, hg), lambda i, j: (0, 0)),
                  pl.BlockSpec((hg * MLA_ROPE, hg), lambda i, j: (0, 0))],
        out_specs=[pl.BlockSpec((tm, hg * MLA_QK_PAD), lambda i, j: (i, j)),
                   pl.BlockSpec((tm, hg * MLA_ROPE), lambda i, j: (i, j))],
        out_shape=[jax.ShapeDtypeStruct((m, MLA_HEADS * MLA_QK_PAD), BF16),
                   jax.ShapeDtypeStruct((m, MLA_HEADS * MLA_ROPE), BF16)],
        scratch_shapes=[pltpu.VMEM((tm, MLA_Q_RANK), BF16)],
        compiler_params=_cparams(("parallel", "arbitrary")),
        name="mla_q",
    )(proj, q_norm_g.reshape(1, -1), w_uq_nope, w_uq_rope, g_q[:MLA_NOPE].reshape(1, -1), gr_t,
      jnp.tile(cos_t, (1, hg)), jnp.tile(sin_t, (1, hg)), head_of(MLA_NOPE), head_of(MLA_ROPE))


def _mla_kv_kernel(ckv_ref, kr_ref, gkv_ref, wuk_ref, wuv_ref, gkn_ref, gkr_ref, cos_ref, sin_ref, swap_ref,
                   lat_ref, kpe_ref, rinv_ref, kc_ref, v_ref, lat_scr, kpe_scr, *, hg):
    j = pl.program_id(1)

    @pl.when(j == 0)
    def _():
        lat = _rms(ckv_ref[...], gkv_ref[...])
        lat_ref[...] = lat
        lat_scr[...] = lat.astype(BF16)
        xg = kr_ref[...] * gkr_ref[...]
        x1, x2, x3 = _split3(xg)
        sw = swap_ref[...]
        swapped = _dot(x1, sw) + _dot(x2, sw) + _dot(x3, sw)
        kpe = xg * cos_ref[...] + swapped * sin_ref[...]
        kpe_ref[...] = kpe
        kpe_scr[...] = kpe
        rinv_ref[...] = jnp.zeros_like(rinv_ref)

    a = lat_scr[...]
    kn = _dot(a, wuk_ref[...])
    v_ref[...] = _dot(a, wuv_ref[...]).astype(v_ref.dtype)
    kr = kr_ref[...]
    ssr = jnp.sum(kr * kr, axis=-1, keepdims=True)
    kpe = kpe_scr[...]
    lane = lax.broadcasted_iota(jnp.int32, rinv_ref.shape, 1)
    racc = rinv_ref[...]
    gkn = gkn_ref[...]
    for h in range(hg):
        ns = slice(h * MLA_NOPE, (h + 1) * MLA_NOPE)
        x = kn[:, ns]
        ri = lax.rsqrt((jnp.sum(x * x, axis=-1, keepdims=True) + ssr) / MLA_QK + EPS)
        base = h * MLA_QK_PAD
        kc_ref[:, base:base + MLA_NOPE] = (x * gkn * ri).astype(kc_ref.dtype)
        kc_ref[:, base + MLA_NOPE:base + MLA_QK] = (kpe * ri).astype(kc_ref.dtype)
        kc_ref[:, base + MLA_QK:base + MLA_QK_PAD] = jnp.zeros((x.shape[0], MLA_QK_PAD - MLA_QK), kc_ref.dtype)
        racc = jnp.where(lane == j * hg + h, ri, racc)
    rinv_ref[...] = racc


def _mla_kv(proj, k_r, kv_norm_g, w_uk2, w_uv2, g_k, cos_t, sin_t):
    m = proj.shape[0]
    hg = MLA_KV_HEADS_PER_STEP
    tm = _tile(m, ROW_TILE, SUBLANE_BF16)
    half = MLA_ROPE // 2
    idx = jnp.arange(MLA_ROPE)
    swap = (idx[:, None] == (idx[None, :] + half) % MLA_ROPE).astype(BF16)
    row = lambda w: pl.BlockSpec((tm, w), lambda i, j: (i, 0))
    vec = lambda w: pl.BlockSpec((1, w), lambda i, j: (0, 0))
    return pl.pallas_call(
        functools.partial(_mla_kv_kernel, hg=hg),
        grid=(m // tm, MLA_HEADS // hg),
        in_specs=[pl.BlockSpec((tm, MLA_KV_RANK), lambda i, j: (i, MLA_Q_RANK // MLA_KV_RANK)),
                  row(MLA_ROPE), vec(MLA_KV_RANK),
                  pl.BlockSpec((MLA_KV_RANK, hg * MLA_NOPE), lambda i, j: (0, j)),
                  pl.BlockSpec((MLA_KV_RANK, hg * MLA_V), lambda i, j: (0, j)),
                  vec(MLA_NOPE), vec(MLA_ROPE), row(MLA_ROPE), row(MLA_ROPE),
                  pl.BlockSpec((MLA_ROPE, MLA_ROPE), lambda i, j: (0, 0))],
        out_specs=[row(MLA_KV_RANK), row(MLA_ROPE), row(MLA_HEADS),
                   pl.BlockSpec((tm, hg * MLA_QK_PAD), lambda i, j: (i, j)),
                   pl.BlockSpec((tm, hg * MLA_V), lambda i, j: (i, j))],
        out_shape=[jax.ShapeDtypeStruct((m, MLA_KV_RANK), F32),
                   jax.ShapeDtypeStruct((m, MLA_ROPE), F32),
                   jax.ShapeDtypeStruct((m, MLA_HEADS), F32),
                   jax.ShapeDtypeStruct((m, MLA_HEADS * MLA_QK_PAD), BF16),
                   jax.ShapeDtypeStruct((m, MLA_HEADS * MLA_V), BF16)],
        scratch_shapes=[pltpu.VMEM((tm, MLA_KV_RANK), BF16), pltpu.VMEM((tm, MLA_ROPE), F32)],
        compiler_params=_cparams(("parallel", "arbitrary")),
        name="mla_kv",
    )(proj, k_r, kv_norm_g.reshape(1, -1), w_uk2, w_uv2, g_k[:MLA_NOPE].reshape(1, -1),
      g_k[MLA_NOPE:].reshape(1, -1), cos_t, sin_t, swap)


def _flash_kernel(q_ref, k_ref, v_ref, o_ref, *, tq, hp):
    i = pl.program_id(2)
    krow = lax.broadcasted_iota(jnp.int32, (tq, tq), 0)
    qcol = lax.broadcasted_iota(jnp.int32, (tq, tq), 1)
    heads = range(hp)
    csl = [slice(h * MLA_QK_PAD, (h + 1) * MLA_QK_PAD) for h in heads]
    vsl = [slice(h * MLA_V, (h + 1) * MLA_V) for h in heads]
    q = [q_ref[:, csl[h]] for h in heads]

    def block(jb, carry, on_diagonal):
        ks = pl.ds(pl.multiple_of(jb * tq, tq), tq)
        s = [_dot_nt(k_ref[ks, csl[h]], q[h]) for h in heads]
        if on_diagonal:
            s = [jnp.where(krow <= qcol, s[h], -jnp.inf) for h in heads]
        m_new = [jnp.maximum(carry[h][0], jnp.max(s[h], axis=0, keepdims=True)) for h in heads]
        corr = [jnp.exp2(carry[h][0] - m_new[h]) for h in heads]
        p = [jnp.exp2(s[h] - m_new[h]) for h in heads]
        l = [carry[h][1] * corr[h] + jnp.sum(p[h], axis=0, keepdims=True) for h in heads]
        acc = [carry[h][2] * corr[h] + _dot_tn(v_ref[ks, vsl[h]], p[h]) for h in heads]
        return tuple((m_new[h], l[h], acc[h]) for h in heads)

    init = tuple((jnp.full((1, tq), -jnp.inf, F32), jnp.zeros((1, tq), F32), jnp.zeros((MLA_V, tq), F32))
                 for _ in heads)
    carry = lax.fori_loop(0, i, lambda jb, c: block(jb, c, False), init)
    carry = block(i, carry, True)
    ri = lax.broadcasted_iota(jnp.int32, (MLA_V, MLA_V), 0)
    ci = lax.broadcasted_iota(jnp.int32, (MLA_V, MLA_V), 1)
    eye = (ri == ci).astype(BF16)
    for h in heads:
        _, l, acc = carry[h]
        o_ref[:, vsl[h]] = _dot_tn((acc / l).astype(BF16), eye).astype(o_ref.dtype)


def _mla_prompt_attention(qc, kc, v, nb, seq):
    hp = FLASH_HEADS_PER_STEP
    tq = _tile(seq, FLASH_BLOCK, SUBLANE_BF16)
    nq = seq // tq
    qmap = lambda b, h, i: (b * nq + i, h)
    kmap = lambda b, h, i: (b, h)
    return pl.pallas_call(
        functools.partial(_flash_kernel, tq=tq, hp=hp),
        grid=(nb, MLA_HEADS // hp, nq),
        in_specs=[pl.BlockSpec((tq, hp * MLA_QK_PAD), qmap),
                  pl.BlockSpec((seq, hp * MLA_QK_PAD), kmap),
                  pl.BlockSpec((seq, hp * MLA_V), kmap)],
        out_specs=pl.BlockSpec((tq, hp * MLA_V), qmap),
        out_shape=jax.ShapeDtypeStruct((nb * seq, MLA_HEADS * MLA_V), BF16),
        compiler_params=_cparams(("parallel", "parallel", "arbitrary")),
        name="mla_prompt_attention",
    )(qc, kc, v)


def _absorb_q_kernel(q_ref, g_ref, w_ref, o_ref):
    o_ref[...] = _dot_nt(q_ref[...].astype(F32) * g_ref[...], w_ref[...]).astype(o_ref.dtype)


def _absorb_q(qc, g_k_nope, w_uk2, mp, ms):
    assert mp % ms == 0
    return pl.pallas_call(
        _absorb_q_kernel,
        grid=(MLA_HEADS,),
        in_specs=[pl.BlockSpec((ms, MLA_NOPE), lambda h: (mp // ms, h * (MLA_QK_PAD // MLA_NOPE))),
                  pl.BlockSpec((1, MLA_NOPE), lambda h: (0, 0)),
                  pl.BlockSpec((MLA_KV_RANK, MLA_NOPE), lambda h: (0, h))],
        out_specs=pl.BlockSpec((ms, MLA_KV_RANK), lambda h: (0, h)),
        out_shape=jax.ShapeDtypeStruct((ms, MLA_HEADS * MLA_KV_RANK), BF16),
        compiler_params=_cparams(("parallel",)),
        name="mla_absorb_q",
    )(qc, g_k_nope.reshape(1, -1), w_uk2)


def _expand_o_kernel(o_ref, w_ref, out_ref):
    out_ref[...] = _dot(o_ref[...], w_ref[...]).astype(out_ref.dtype)


def _expand_o(o_lat, w_uv2):
    ms = o_lat.shape[0]
    return pl.pallas_call(
        _expand_o_kernel,
        grid=(MLA_HEADS,),
        in_specs=[pl.BlockSpec((ms, MLA_KV_RANK), lambda h: (0, h)),
                  pl.BlockSpec((MLA_KV_RANK, MLA_V), lambda h: (0, h))],
        out_specs=pl.BlockSpec((ms, MLA_V), lambda h: (0, h)),
        out_shape=jax.ShapeDtypeStruct((ms, MLA_HEADS * MLA_V), BF16),
        compiler_params=_cparams(("parallel",)),
        name="mla_expand_o",
    )(o_lat, w_uv2)


def _softmax_update(sc, values, m_scr, l_scr, acc_scr):
    m_old = m_scr[...]
    m_new = jnp.maximum(m_old, jnp.max(sc, axis=-1, keepdims=True))
    corr = jnp.exp2(m_old - m_new)
    p = jnp.exp2(sc - m_new)
    l_scr[...] = l_scr[...] * corr + jnp.sum(p, axis=-1, keepdims=True)
    acc_scr[...] = acc_scr[...] * corr + _dot(p, values)
    m_scr[...] = m_new


def _sample_attn_kernel(pt_ref, ql_ref, qp_ref, lat_hbm, pe_hbm, ri_hbm, latn_ref, pen_ref, rin_ref, o_ref,
                        m_scr, l_scr, acc_scr, lat_raw, pe_raw, ri_raw, lat_buf, pe_buf, sems,
                        *, li, pps, nsteps, total_steps, steps, page):
    s = pl.program_id(1)
    t = pl.program_id(0) * nsteps + s
    slot = t % 2

    def page_copies(step, buf_slot):
        out = []
        for p in range(pps):
            pg = pt_ref[step * pps + p]
            keys = pl.ds(p * page, page)
            out.append(pltpu.make_async_copy(lat_hbm.at[li, pg], lat_raw.at[buf_slot, keys, :], sems.at[buf_slot, 0]))
            out.append(pltpu.make_async_copy(pe_hbm.at[li, pg], pe_raw.at[buf_slot, :, keys], sems.at[buf_slot, 1]))
            out.append(pltpu.make_async_copy(ri_hbm.at[li, pg], ri_raw.at[buf_slot, :, keys], sems.at[buf_slot, 2]))
        return out

    @pl.when(t == 0)
    def _():
        for c in page_copies(0, 0):
            c.start()

    @pl.when(t + 1 < total_steps)
    def _():
        for c in page_copies(t + 1, 1 - slot):
            c.start()

    @pl.when(s == 0)
    def _():
        m_scr[...] = jnp.full_like(m_scr, -jnp.inf)
        l_scr[...] = jnp.zeros_like(l_scr)
        acc_scr[...] = jnp.zeros_like(acc_scr)

    for c in page_copies(t, slot):
        c.wait()
    lat_buf[...] = lat_raw[slot].astype(BF16)
    pe_buf[...] = pe_raw[slot].astype(BF16)
    ql = ql_ref[...]
    qp = qp_ref[...]
    nsub = SAMPLE_SUBBLOCKS if pps % SAMPLE_SUBBLOCKS == 0 else 1
    width = pps * page // nsub
    subs = [pl.ds(u * width, width) for u in range(nsub)]
    lat = [lat_buf[sl, :] for sl in subs]
    sc = [(_dot_nt(ql, lat[u]) + _dot(qp, pe_buf[:, subs[u]])) * jnp.concatenate([ri_raw[slot, :, subs[u]]] * steps, axis=0)
          for u in range(nsub)]
    for u in range(nsub):
        _softmax_update(sc[u], lat[u], m_scr, l_scr, acc_scr)

    @pl.when(s == nsteps - 1)
    def _():
        latn = latn_ref[...].astype(BF16)
        scn = (_dot_nt(ql, latn) + _dot(qp, pen_ref[...])) * jnp.concatenate([rin_ref[...]] * steps, axis=0)
        qstep = lax.broadcasted_iota(jnp.int32, scn.shape, 0) // MLA_HEADS
        kstep = lax.broadcasted_iota(jnp.int32, scn.shape, 1)
        scn = jnp.where(jnp.logical_and(kstep < steps, kstep <= qstep), scn, -jnp.inf)
        _softmax_update(scn, latn, m_scr, l_scr, acc_scr)
        o_ref[...] = (acc_scr[...] / l_scr[...]).astype(o_ref.dtype)


def _mla_sample_attention(q_lat, q_pe, lat_new, pe_new_t, rinv_new_t, cache_lat, cache_pe_t, cache_rinv_t, li,
                          page_table, steps):
    nb, n_pages = page_table.shape
    page = cache_lat.shape[2]
    pps = _tile(n_pages, PAGES_PER_STEP, 1)
    nsteps = n_pages // pps
    rows = q_lat.shape[1]
    npad = lat_new.shape[1]
    per_b = lambda r, w: pl.BlockSpec((None, r, w), lambda b, s, pt: (b, 0, 0))
    hbm = pl.BlockSpec(memory_space=pl.ANY)
    grid_spec = pltpu.PrefetchScalarGridSpec(
        num_scalar_prefetch=1,
        grid=(nb, nsteps),
        in_specs=[per_b(rows, MLA_KV_RANK), per_b(rows, MLA_ROPE), hbm, hbm, hbm,
                  per_b(npad, MLA_KV_RANK), per_b(MLA_ROPE, npad), per_b(MLA_HEADS, npad)],
        out_specs=per_b(rows, MLA_KV_RANK),
        scratch_shapes=[pltpu.VMEM((rows, 1), F32), pltpu.VMEM((rows, 1), F32),
                        pltpu.VMEM((rows, MLA_KV_RANK), F32),
                        pltpu.VMEM((2, pps * page, MLA_KV_RANK), F32),
                        pltpu.VMEM((2, MLA_ROPE, pps * page), F32),
                        pltpu.VMEM((2, MLA_HEADS, pps * page), F32),
                        pltpu.VMEM((pps * page, MLA_KV_RANK), BF16),
                        pltpu.VMEM((MLA_ROPE, pps * page), BF16),
                        pltpu.SemaphoreType.DMA((2, 3))])
    return pl.pallas_call(
        functools.partial(_sample_attn_kernel, li=li, pps=pps, nsteps=nsteps, total_steps=nb * nsteps, steps=steps,
                          page=page),
        grid_spec=grid_spec,
        out_shape=jax.ShapeDtypeStruct((nb, rows, MLA_KV_RANK), BF16),
        compiler_params=_cparams(("arbitrary", "arbitrary")),
        name="mla_sample_attention",
    )(page_table.reshape(-1), q_lat, q_pe, cache_lat, cache_pe_t, cache_rinv_t, lat_new, pe_new_t.astype(BF16),
      rinv_new_t)


def _rope_tables(pos):
    half = MLA_ROPE // 2
    inv = ROPE_THETA ** (-jnp.arange(half, dtype=F32) / half)
    ang = pos.astype(F32)[:, None] * inv[None, :]
    cos, sin = jnp.cos(ang), jnp.sin(ang)
    return jnp.concatenate([cos, cos], axis=-1), jnp.concatenate([-sin, sin], axis=-1)


def _mla_mixer(h, x_res, cache_lat, cache_pe, cache_rinv, page_table, w_in, q_norm_g, kv_norm_g, w_uq, w_uk,
               w_uv, g_q, g_k, w_o, li, dims, past_len):
    mp, ms, nbp, seq, nbs, steps = dims
    pos = jnp.concatenate([jnp.tile(jnp.arange(seq), nbp), jnp.tile(past_len + jnp.arange(steps), nbs)])
    cos_t, sin_t = _rope_tables(pos)
    proj = _mm(h, w_in, (li,), name="mla_in_proj")
    w_uq3 = w_uq[li].reshape(MLA_Q_RANK, MLA_HEADS, MLA_QK)
    qc, qr = _mla_q(proj, q_norm_g[li], w_uq3[:, :, :MLA_NOPE].reshape(MLA_Q_RANK, -1),
                    w_uq3[:, :, MLA_NOPE:].reshape(MLA_Q_RANK, -1), g_q[li], cos_t, sin_t)
    w_uk2 = w_uk[li].reshape(MLA_KV_RANK, -1)
    w_uv2 = w_uv[li].reshape(MLA_KV_RANK, -1)
    lat, kpe, rinv, kc, v = _mla_kv(proj, proj[:, MLA_Q_RANK + MLA_KV_RANK:], kv_norm_g[li], w_uk2, w_uv2,
                                         g_k[li], cos_t, sin_t)
    o_p = _mla_prompt_attention(qc, kc, v, nbp, seq)

    q_lat = _absorb_q(qc, g_k[li, :MLA_NOPE], w_uk2, mp, ms)
    padn = lambda a: jnp.pad(a[mp:].reshape(nbs, steps, -1), ((0, 0), (0, 8 - steps), (0, 0)))
    o_lat = _mla_sample_attention(q_lat.reshape(nbs, steps * MLA_HEADS, MLA_KV_RANK),
                                  qr[mp:].reshape(nbs, steps * MLA_HEADS, MLA_ROPE),
                                  padn(lat), padn(kpe).swapaxes(1, 2), padn(rinv).swapaxes(1, 2), cache_lat,
                                  cache_pe.swapaxes(2, 3), cache_rinv.swapaxes(2, 3), li, page_table, steps)
    o_s = _expand_o(o_lat.reshape(ms, MLA_HEADS * MLA_KV_RANK), w_uv2)
    x_new = _mm(jnp.concatenate([o_p, o_s], axis=0), w_o, (li,), residual=x_res, tk=2048, name="mla_out_proj")
    return x_new, lat, kpe, rinv


def kernel(x_prompt, x_sample, state_gdn_ssm, state_gdn_conv, cache_mla_latent, cache_mla_k_pe, cache_mla_k_rinv,
           page_table, norm_g, ffn_w_gate, ffn_w_up, ffn_w_down, gdn_w_in, gdn_conv_w, gdn_a_log, gdn_dt_bias,
           gdn_norm_g, gdn_w_out, mla_w_in, mla_q_norm_g, mla_kv_norm_g, mla_w_uq, mla_w_uk, mla_w_uv,
           mla_qk_norm_q, mla_qk_norm_k, mla_w_o):
    nbp, seq, d = x_prompt.shape
    nbs, steps, _ = x_sample.shape
    mp, ms = nbp * seq, nbs * steps
    dims = (mp, ms, nbp, seq, nbs, steps)
    depth = norm_g.shape[0]
    past_len = page_table.shape[1] * cache_mla_latent.shape[2]
    x = jnp.concatenate([x_prompt.reshape(mp, d), x_sample.reshape(ms, d)], axis=0)
    w_down_bf16 = ffn_w_down.astype(BF16)
    h = _norm(x, norm_g[0, 0], BF16)
    outs = {k: [] for k in ("ssm_p", "conv_p", "ssm_s", "conv_s", "lat", "pe", "rinv")}
    for layer in range(depth):
        li = layer // 2
        x = _macaron_half(x, h, ffn_w_gate, ffn_w_up, w_down_bf16, layer, 0)
        h = _norm(x, norm_g[layer, 1], BF16)
        if layer % 2 == 0:
            x, ssm_p, conv_p, ssm_s, conv_s = _gdn_mixer(
                h, x, state_gdn_ssm, state_gdn_conv, gdn_w_in, gdn_conv_w, gdn_a_log, gdn_dt_bias, gdn_norm_g,
                gdn_w_out, li, dims)
            for key, val in (("ssm_p", ssm_p), ("conv_p", conv_p), ("ssm_s", ssm_s), ("conv_s", conv_s)):
                outs[key].append(val)
        else:
            x, lat, kpe, rinv = _mla_mixer(
                h, x, cache_mla_latent, cache_mla_k_pe, cache_mla_k_rinv, page_table, mla_w_in, mla_q_norm_g,
                mla_kv_norm_g, mla_w_uq, mla_w_uk, mla_w_uv, mla_qk_norm_q, mla_qk_norm_k, mla_w_o, li, dims,
                past_len)
            for key, val in (("lat", lat), ("pe", kpe), ("rinv", rinv)):
                outs[key].append(val)
        h = _norm(x, norm_g[layer, 2], BF16)
        x = _macaron_half(x, h, ffn_w_gate, ffn_w_up, w_down_bf16, layer, 1)
        if layer + 1 < depth:
            x, h = _norm2(x, norm_g[layer, 3], norm_g[layer + 1, 0])
        else:
            y_p = _norm(x, norm_g[layer, 3], F32, 0, mp).reshape(nbp, seq, d)
            y_s = _norm(x, norm_g[layer, 3], F32, mp, ms).reshape(nbs, steps, d)

    def split(a):
        return a[:mp].reshape(nbp, seq, -1), a[mp:].reshape(nbs, steps, -1)

    lat_p, lat_s = zip(*[split(a) for a in outs["lat"]])
    pe_p, pe_s = zip(*[split(a) for a in outs["pe"]])
    rinv_p, rinv_s = zip(*[split(a) for a in outs["rinv"]])
    return (y_p, y_s,
            jnp.stack(outs["ssm_p"]), jnp.stack(outs["conv_p"]), jnp.stack(lat_p), jnp.stack(pe_p),
            jnp.stack(rinv_p),
            jnp.stack(outs["ssm_s"]), jnp.stack(outs["conv_s"]), jnp.stack(lat_s), jnp.stack(pe_s),
            jnp.stack(rinv_s))
```
